```python
import math
import jax
import jax.numpy as jnp
from jax import lax
import numpy as np

D_MODEL = 1024
BATCH = 16
SEQ = 2048
DEPTH = 2

CHUNK = 64
Q_BLOCK = 128
N_MIXERS = 2
N_HEADS = 16
HEAD_DIM = 64
D_ATT = N_HEADS * HEAD_DIM
SSM_GROUP = 16
D_SSM = D_MODEL
N_GROUPS = D_SSM // SSM_GROUP
STATE = 64
_FF_RAW = -(-8 * D_MODEL // 3)
D_FF = -(-_FF_RAW // 256) * 256
N_FOX = (DEPTH + 1) // 2
N_S5 = DEPTH // 2
EPS = 1e-6
DT_MIN = 1e-3
DT_MAX = 1e-1

kernel_name = 'hybrid_fox_s5_sandwich_trunk'


def rmsnorm(x, gain):
    xf = x.astype(jnp.float32)
    y = xf * lax.rsqrt(jnp.mean(xf * xf, axis=-1, keepdims=True) + EPS)
    return (y * gain.astype(jnp.float32)).astype(x.dtype)


def fox_mixer(h, w_in, b_f, q_gain, k_gain, w_out):
    bsz, seq_len, _ = h.shape
    proj = h @ w_in
    q = proj[..., :D_ATT]
    k = proj[..., D_ATT:2 * D_ATT]
    v = proj[..., 2 * D_ATT:3 * D_ATT]
    gate = proj[..., 3 * D_ATT:4 * D_ATT]
    f_logit = proj[..., 4 * D_ATT:]

    def heads(t):
        return t.reshape(bsz, seq_len, N_HEADS, HEAD_DIM).transpose(0, 2, 1, 3)

    q = rmsnorm(heads(q), q_gain)
    k = rmsnorm(heads(k), k_gain)
    v = heads(v)
    log_f = jax.nn.log_sigmoid(f_logit.astype(jnp.float32) + b_f.astype(jnp.float32))
    c = jnp.cumsum(log_f, axis=1).transpose(0, 2, 1)
    scale = HEAD_DIM ** -0.5

    outs = []
    for blk in range(seq_len // Q_BLOCK):
        q0 = blk * Q_BLOCK
        k_end = q0 + Q_BLOCK
        qb = q[:, :, q0:k_end]
        kb = k[:, :, :k_end]
        vb = v[:, :, :k_end]
        s = jnp.einsum('bhqd,bhkd->bhqk', qb, kb).astype(jnp.float32) * scale
        s = s + c[:, :, q0:k_end, None] - c[:, :, None, :k_end]
        q_pos = q0 + jnp.arange(Q_BLOCK)
        mask = q_pos[:, None] >= jnp.arange(k_end)[None, :]
        s = jnp.where(mask, s, -jnp.inf)
        p = jax.nn.softmax(s, axis=-1).astype(vb.dtype)
        outs.append(jnp.einsum('bhqk,bhkd->bhqd', p, vb))
    o = jnp.concatenate(outs, axis=2)
    o = o.transpose(0, 2, 1, 3).reshape(bsz, seq_len, D_ATT)
    o = o * jax.nn.sigmoid(gate)
    return o @ w_out


def _complex_linear_combine(e1, e2):
    a1r, a1i, b1r, b1i = e1
    a2r, a2i, b2r, b2i = e2
    return (a2r * a1r - a2i * a1i,
            a2r * a1i + a2i * a1r,
            a2r * b1r - a2i * b1i + b2r,
            a2r * b1i + a2i * b1r + b2i)


def s5_mixer(h, w_in, log_dt, lam_re, lam_im, b_re, b_im, c_re, c_im, d_skip, w_glu, w_out):
    f32 = jnp.float32
    bsz, seq_len, _ = h.shape
    u = (h @ w_in).astype(f32)
    lam_re = lam_re.astype(f32)
    lam_im = lam_im.astype(f32)
    dt = jnp.exp(log_dt.astype(f32))[:, None]
    mag = jnp.exp(lam_re * dt)
    a_re = mag * jnp.cos(lam_im * dt)
    a_im = mag * jnp.sin(lam_im * dt)
    den = lam_re * lam_re + lam_im * lam_im
    n_re = a_re - 1.0
    z_re = (n_re * lam_re + a_im * lam_im) / den
    z_im = (a_im * lam_re - n_re * lam_im) / den
    b_re = b_re.astype(f32)
    b_im = b_im.astype(f32)
    bb_re = z_re[..., None] * b_re - z_im[..., None] * b_im
    bb_im = z_re[..., None] * b_im + z_im[..., None] * b_re
    c_re = c_re.astype(f32)
    c_im = c_im.astype(f32)

    n_chunks = seq_len // CHUNK
    u_chunks = u.reshape(bsz, n_chunks, CHUNK, N_GROUPS, SSM_GROUP).transpose(1, 0, 2, 3, 4)

    def step(carry, u_c):
        s_re, s_im = carry
        bu_re = jnp.einsum('gpc,bkgc->bkgp', bb_re, u_c)
        bu_im = jnp.einsum('gpc,bkgc->bkgp', bb_im, u_c)
        bu_re = bu_re.at[:, 0].add(a_re * s_re - a_im * s_im)
        bu_im = bu_im.at[:, 0].add(a_re * s_im + a_im * s_re)
        a_seq_re = jnp.broadcast_to(a_re, bu_re.shape)
        a_seq_im = jnp.broadcast_to(a_im, bu_im.shape)
        _, _, x_re, x_im = lax.associative_scan(
            _complex_linear_combine, (a_seq_re, a_seq_im, bu_re, bu_im), axis=1)
        y = (jnp.einsum('gcp,bkgp->bkgc', c_re, x_re)
             - jnp.einsum('gcp,bkgp->bkgc', c_im, x_im))
        return (x_re[:, -1], x_im[:, -1]), y

    init = (jnp.zeros((bsz, N_GROUPS, STATE), f32), jnp.zeros((bsz, N_GROUPS, STATE), f32))
    _, y = lax.scan(step, init, u_chunks)
    y = y.transpose(1, 0, 2, 3, 4).reshape(bsz, seq_len, D_SSM)
    y = y + d_skip.astype(f32) * u
    y = jax.nn.gelu(y).astype(h.dtype)
    y = y * jax.nn.sigmoid(y @ w_glu)
    return y @ w_out


def swiglu(h, w_gate, w_up, w_down):
    return (jax.nn.silu(h @ w_gate) * (h @ w_up)) @ w_down


def setup_inputs(seed: int = 0) -> dict:
    key = jax.random.key(seed)
    ks = jax.random.split(key, 24)
    f32 = jnp.float32

    def nrm(k, shape, scale):
        return jax.random.normal(k, shape, f32) * scale

    def gain(k, shape):
        return 1.0 + 0.02 * jax.random.normal(k, shape, f32)

    x = jax.random.normal(ks[0], (BATCH, SEQ, D_MODEL), f32)
    fox_w_in = nrm(ks[1], (N_FOX, D_MODEL, 4 * D_ATT + N_HEADS), D_MODEL ** -0.5)
    fox_b_f = jax.random.uniform(ks[2], (N_FOX, N_HEADS), f32, minval=1.0, maxval=5.0)
    fox_q_gain = gain(ks[3], (N_FOX, HEAD_DIM))
    fox_k_gain = gain(ks[4], (N_FOX, HEAD_DIM))
    fox_w_out = nrm(ks[5], (N_FOX, D_ATT, D_MODEL), D_ATT ** -0.5)
    s5_w_in = nrm(ks[6], (N_S5, D_MODEL, D_SSM), D_MODEL ** -0.5)
    s5_log_dt = jax.random.uniform(ks[7], (N_S5, N_GROUPS), f32,
                                   minval=math.log(DT_MIN), maxval=math.log(DT_MAX))
    s5_lam_re = -0.5 + 0.01 * jax.random.normal(ks[8], (N_S5, N_GROUPS, STATE), f32)
    s5_lam_im = (math.pi * jnp.arange(STATE, dtype=f32)
                 + 0.01 * jax.random.normal(ks[9], (N_S5, N_GROUPS, STATE), f32))
    s5_b_re = nrm(ks[10], (N_S5, N_GROUPS, STATE, SSM_GROUP), (2 * SSM_GROUP) ** -0.5)
    s5_b_im = nrm(ks[11], (N_S5, N_GROUPS, STATE, SSM_GROUP), (2 * SSM_GROUP) ** -0.5)
    s5_c_re = nrm(ks[12], (N_S5, N_GROUPS, SSM_GROUP, STATE), (2 * STATE) ** -0.5)
    s5_c_im = nrm(ks[13], (N_S5, N_GROUPS, SSM_GROUP, STATE), (2 * STATE) ** -0.5)
    s5_d = nrm(ks[14], (N_S5, D_SSM), 1.0)
    s5_w_glu = nrm(ks[15], (N_S5, D_SSM, D_SSM), D_SSM ** -0.5)
    s5_w_out = nrm(ks[16], (N_S5, D_SSM, D_MODEL), D_SSM ** -0.5)
    mix_pre_gain = gain(ks[17], (DEPTH, D_MODEL))
    mix_post_gain = gain(ks[18], (DEPTH, D_MODEL))
    ffn_pre_gain = gain(ks[19], (DEPTH, D_MODEL))
    ffn_post_gain = gain(ks[20], (DEPTH, D_MODEL))
    ffn_w_gate = nrm(ks[21], (DEPTH, D_MODEL, D_FF), D_MODEL ** -0.5)
    ffn_w_up = nrm(ks[22], (DEPTH, D_MODEL, D_FF), D_MODEL ** -0.5)
    ffn_w_down = nrm(ks[23], (DEPTH, D_FF, D_MODEL), D_FF ** -0.5)
    return {
        'x': x,
        'fox_w_in': fox_w_in, 'fox_b_f': fox_b_f, 'fox_q_gain': fox_q_gain,
        'fox_k_gain': fox_k_gain, 'fox_w_out': fox_w_out,
        's5_w_in': s5_w_in, 's5_log_dt': s5_log_dt, 's5_lam_re': s5_lam_re,
        's5_lam_im': s5_lam_im, 's5_b_re': s5_b_re, 's5_b_im': s5_b_im,
        's5_c_re': s5_c_re, 's5_c_im': s5_c_im, 's5_d': s5_d,
        's5_w_glu': s5_w_glu, 's5_w_out': s5_w_out,
        'mix_pre_gain': mix_pre_gain, 'mix_post_gain': mix_post_gain,
        'ffn_pre_gain': ffn_pre_gain, 'ffn_post_gain': ffn_post_gain,
        'ffn_w_gate': ffn_w_gate, 'ffn_w_up': ffn_w_up, 'ffn_w_down': ffn_w_down,
    }


def reference(x, fox_w_in, fox_b_f, fox_q_gain, fox_k_gain, fox_w_out,
              s5_w_in, s5_log_dt, s5_lam_re, s5_lam_im, s5_b_re, s5_b_im,
              s5_c_re, s5_c_im, s5_d, s5_w_glu, s5_w_out,
              mix_pre_gain, mix_post_gain, ffn_pre_gain, ffn_post_gain,
              ffn_w_gate, ffn_w_up, ffn_w_down):
    for i in range(DEPTH):
        j = i // N_MIXERS
        h = rmsnorm(x, mix_pre_gain[i])
        if i % N_MIXERS == 0:
            m = fox_mixer(h, fox_w_in[j], fox_b_f[j], fox_q_gain[j], fox_k_gain[j], fox_w_out[j])
        else:
            m = s5_mixer(h, s5_w_in[j], s5_log_dt[j], s5_lam_re[j], s5_lam_im[j],
                         s5_b_re[j], s5_b_im[j], s5_c_re[j], s5_c_im[j], s5_d[j],
                         s5_w_glu[j], s5_w_out[j])
        x = x + rmsnorm(m, mix_post_gain[i])
        h = rmsnorm(x, ffn_pre_gain[i])
        x = x + rmsnorm(swiglu(h, ffn_w_gate[i], ffn_w_up[i], ffn_w_down[i]), ffn_post_gain[i])
    return x
```

```python
import functools
import math

import jax
import jax.numpy as jnp
from jax import lax
from jax.experimental import pallas as pl
from jax.experimental.pallas import tpu as pltpu

F32 = jnp.float32
BF16 = jnp.bfloat16

D_MODEL = 1024
BATCH = 16
SEQ = 2048
N_TOK = BATCH * SEQ
N_HEADS = 16
HEAD_DIM = 64
D_ATT = N_HEADS * HEAD_DIM
SSM_GROUP = 16
N_GROUPS = 64
STATE = 64
D_FF = 2816
EPS = 1e-6

LANES = 128
ROW_TILE = 512
TILES_PER_SEQ = SEQ // ROW_TILE
ATT_TQ = 256
ATT_TK = 256
HEAD_PAIRS = N_HEADS // 2
SCAN_STEPS = 32
SCAN_ROWS = SCAN_STEPS * BATCH
GROUP_BLOCK = 16
N_GROUP_BLOCKS = N_GROUPS // GROUP_BLOCK
GB_CH = GROUP_BLOCK * SSM_GROUP
GB_ST = GROUP_BLOCK * STATE
FF_CHUNK = D_FF // 2
VMEM_LIMIT = 56 * 1024 * 1024


def _const_spec(shape):
    nd = len(shape)
    return pl.BlockSpec(shape, lambda *_: (0,) * nd, pipeline_mode=pl.Buffered(1))


def _rmsnorm(xf, gain):
    return xf * lax.rsqrt(jnp.mean(xf * xf, axis=-1, keepdims=True) + EPS) * gain


def _dot(a, b):
    return jnp.dot(a, b, preferred_element_type=F32)


def _sigmoid(x):
    return 1.0 / (1.0 + jnp.exp(-x))


def _params(*sem):
    return pltpu.CompilerParams(dimension_semantics=sem, vmem_limit_bytes=VMEM_LIMIT)


def _fox_in_kernel(x_ref, g_ref, w_ref, wf_ref, bf_ref, qg_ref, kg_ref, e_ref,
                   q_ref, k_ref, v_ref, sg_ref, c_ref, carry_ref):
    i = pl.program_id(0)
    h = _rmsnorm(x_ref[...], g_ref[...]).astype(BF16)

    for dst, col0, gain in ((q_ref, 0, qg_ref), (k_ref, D_ATT, kg_ref)):
        for cg in range(D_ATT // 256):
            lo = cg * 256
            blk = _dot(h, w_ref[:, col0 + lo:col0 + lo + 256])
            ssq = _dot((blk * blk).astype(BF16), e_ref[...])
            dst[:, lo:lo + 256] = (
                blk * lax.rsqrt(ssq * (1.0 / HEAD_DIM) + EPS) * gain[...]).astype(BF16)

    v_ref[...] = _dot(h, w_ref[:, 2 * D_ATT:3 * D_ATT]).astype(BF16)
    sg_ref[...] = _sigmoid(_dot(h, w_ref[:, 3 * D_ATT:4 * D_ATT])).astype(BF16)

    z = _dot(h, wf_ref[...]) + bf_ref[...]
    log_f = jnp.minimum(z, 0.0) - jnp.log(1.0 + jnp.exp(-jnp.abs(z)))
    rows = lax.broadcasted_iota(jnp.int32, (ROW_TILE, ROW_TILE), 0)
    cols = lax.broadcasted_iota(jnp.int32, (ROW_TILE, ROW_TILE), 1)
    tril = (rows >= cols).astype(BF16)
    hi = log_f.astype(BF16)
    r1 = log_f - hi.astype(F32)
    mid = r1.astype(BF16)
    lo3 = (r1 - mid.astype(F32)).astype(BF16)
    cum = _dot(tril, hi) + _dot(tril, mid) + _dot(tril, lo3)

    @pl.when(i % TILES_PER_SEQ == 0)
    def _():
        carry_ref[...] = jnp.zeros_like(carry_ref)

    cum = cum + carry_ref[...]
    c_ref[...] = cum
    carry_ref[...] = cum[ROW_TILE - 1:ROW_TILE, :]


def _fox_in(x, pre_gain, w_main, w_f, b_f, q_gain, k_gain, e_blk):
    row_spec = pl.BlockSpec((ROW_TILE, D_MODEL), lambda i: (i, 0))
    act = jax.ShapeDtypeStruct((N_TOK, D_ATT), BF16)
    return pl.pallas_call(
        _fox_in_kernel,
        grid=(N_TOK // ROW_TILE,),
        in_specs=[
            row_spec,
            _const_spec((1, D_MODEL)),
            _const_spec((D_MODEL, 4 * D_ATT)),
            _const_spec((D_MODEL, LANES)),
            _const_spec((1, LANES)),
            _const_spec((1, 256)),
            _const_spec((1, 256)),
            _const_spec((256, 256)),
        ],
        out_specs=[row_spec, row_spec, row_spec, row_spec,
                   pl.BlockSpec((ROW_TILE, LANES), lambda i: (i, 0))],
        out_shape=[act, act, act, act, jax.ShapeDtypeStruct((N_TOK, LANES), F32)],
        scratch_shapes=[pltpu.VMEM((1, LANES), F32)],
        compiler_params=_params("arbitrary"),
        name="fox_in",
    )(x, pre_gain, w_main, w_f, b_f, q_gain, k_gain, e_blk)


def _attn_kernel(q_ref, k_ref, v_ref, ck_ref, o_ref):
    qi = pl.program_id(2)
    lane = lax.broadcasted_iota(jnp.int32, (ATT_TQ, LANES), 1)
    low_half = lane < HEAD_DIM
    q = q_ref[...].astype(F32)
    row = lax.broadcasted_iota(jnp.int32, (ATT_TQ, ATT_TK), 0)
    col = lax.broadcasted_iota(jnp.int32, (ATT_TQ, ATT_TK), 1)
    causal = row >= col

    outs = []
    for hh in range(2):
        keep = low_half if hh == 0 else jnp.logical_not(low_half)
        qm = jnp.where(keep, q, 0.0).astype(BF16)

        def block(kb, carry, masked, qm=qm, hh=hh):
            m, l, acc = carry
            k0 = pl.multiple_of(kb * ATT_TK, ATT_TK)
            kblk = k_ref[pl.ds(k0, ATT_TK), :]
            s = lax.dot_general(qm, kblk, (((1,), (1,)), ((), ())),
                                preferred_element_type=F32)
            s = s - ck_ref[hh:hh + 1, pl.ds(k0, ATT_TK)]
            if masked:
                s = jnp.where(causal, s, -1e30)
            m_new = jnp.maximum(m, jnp.max(s, axis=-1, keepdims=True))
            alpha = jnp.exp(m - m_new)
            p = jnp.exp(s - m_new)
            l = alpha * l + jnp.sum(p, axis=-1, keepdims=True)
            acc = alpha * acc + _dot(p.astype(BF16), v_ref[pl.ds(k0, ATT_TK), :])
            return m_new, l, acc

        init = (jnp.full((ATT_TQ, 1), -1e30, F32), jnp.zeros((ATT_TQ, 1), F32),
                jnp.zeros((ATT_TQ, LANES), F32))
        carry = lax.fori_loop(0, qi, functools.partial(block, masked=False), init)
        _, l, acc = block(qi, carry, True)
        outs.append(acc / l)

    o_ref[...] = jnp.where(low_half, outs[0], outs[1]).astype(BF16)


def _attention(q, k, v, ck):
    nq = SEQ // ATT_TQ
    q_spec = pl.BlockSpec((ATT_TQ, LANES), lambda b, j, qi: (b * nq + qi, j))
    kv_spec = pl.BlockSpec((SEQ, LANES), lambda b, j, qi: (b, j))
    return pl.pallas_call(
        _attn_kernel,
        grid=(BATCH, HEAD_PAIRS, nq),
        in_specs=[q_spec, kv_spec, kv_spec,
                  pl.BlockSpec((None, None, 2, SEQ), lambda b, j, qi: (b, j, 0, 0))],
        out_specs=q_spec,
        out_shape=jax.ShapeDtypeStruct((N_TOK, D_ATT), BF16),
        compiler_params=_params("arbitrary", "arbitrary", "arbitrary"),
        name="fox_attention",
    )(q, k, v, ck)


def _fox_out_kernel(o_ref, sg_ref, x_ref, w_ref, g_ref, y_ref):
    m = _dot(o_ref[...] * sg_ref[...], w_ref[...])
    y_ref[...] = x_ref[...] + _rmsnorm(m, g_ref[...])


def _fox_out(o, sg, x, w_out, post_gain):
    row_spec = pl.BlockSpec((ROW_TILE, D_MODEL), lambda i: (i, 0))
    return pl.pallas_call(
        _fox_out_kernel,
        grid=(N_TOK // ROW_TILE,),
        in_specs=[row_spec, row_spec, row_spec,
                  _const_spec((D_ATT, D_MODEL)), _const_spec((1, D_MODEL))],
        out_specs=row_spec,
        out_shape=jax.ShapeDtypeStruct((N_TOK, D_MODEL), F32),
        compiler_params=_params("arbitrary"),
        name="fox_out",
    )(o, sg, x, w_out, post_gain)


def _ffn_kernel(x_ref, pre_ref, wg_ref, wu_ref, wd_ref, post_ref, y_ref):
    x = x_ref[...]
    h = _rmsnorm(x, pre_ref[...]).astype(BF16)
    acc = None
    for c0 in range(0, D_FF, FF_CHUNK):
        g = _dot(h, wg_ref[:, c0:c0 + FF_CHUNK])
        u = _dot(h, wu_ref[:, c0:c0 + FF_CHUNK])
        a = (g * _sigmoid(g) * u).astype(BF16)
        part = _dot(a, wd_ref[c0:c0 + FF_CHUNK, :])
        acc = part if acc is None else acc + part
    y_ref[...] = x + _rmsnorm(acc, post_ref[...])


def _batch_major_spec():
    return pl.BlockSpec((ROW_TILE, D_MODEL), lambda i: (i, 0))


def _time_major_spec():
    return pl.BlockSpec((ROW_TILE, D_MODEL),
                        lambda i: (i % TILES_PER_SEQ, i // TILES_PER_SEQ))


def _ffn(x, pre_gain, w_gate, w_up, w_down, post_gain, *, to_time_major):
    if to_time_major:
        in_spec, out_spec = _batch_major_spec(), _time_major_spec()
        out_shape = (SEQ, BATCH * D_MODEL)
    else:
        in_spec, out_spec = _time_major_spec(), _batch_major_spec()
        out_shape = (N_TOK, D_MODEL)
        x = x.reshape(SEQ, BATCH * D_MODEL)
    return pl.pallas_call(
        _ffn_kernel,
        grid=(N_TOK // ROW_TILE,),
        in_specs=[in_spec, _const_spec((1, D_MODEL)),
                  _const_spec((D_MODEL, D_FF)), _const_spec((D_MODEL, D_FF)),
                  _const_spec((D_FF, D_MODEL)), _const_spec((1, D_MODEL))],
        out_specs=out_spec,
        out_shape=jax.ShapeDtypeStruct(out_shape, F32),
        compiler_params=_params("arbitrary"),
        name="ffn_to_time_major" if to_time_major else "ffn_to_batch_major",
    )(x, pre_gain, w_gate, w_up, w_down, post_gain)


def _s5_disc_kernel(ldt_ref, lre_ref, lim_ref, bre_ref, bim_ref,
                    are_ref, aim_ref, bbre_ref, bbim_ref):
    dt = jnp.exp(ldt_ref[...])
    lam_re = lre_ref[...]
    lam_im = lim_ref[...]
    mag = jnp.exp(lam_re * dt)
    a_re = mag * jnp.cos(lam_im * dt)
    a_im = mag * jnp.sin(lam_im * dt)
    den = lam_re * lam_re + lam_im * lam_im
    n_re = a_re - 1.0
    z_re = (n_re * lam_re + a_im * lam_im) / den
    z_im = (a_im * lam_re - n_re * lam_im) / den
    are_ref[...] = a_re
    aim_ref[...] = a_im
    b_re = bre_ref[...]
    b_im = bim_ref[...]
    bbre_ref[...] = z_re * b_re - z_im * b_im
    bbim_ref[...] = z_re * b_im + z_im * b_re


def _s5_discretise(log_dt, lam_re, lam_im, b_re_t, b_im_t):
    gs = jax.ShapeDtypeStruct((N_GROUPS, 1, STATE), F32)
    gcs = jax.ShapeDtypeStruct((N_GROUPS, SSM_GROUP, STATE), F32)
    return pl.pallas_call(
        _s5_disc_kernel,
        out_shape=[gs, gs, gcs, gcs],
        name="s5_discretise",
    )(log_dt, lam_re, lam_im, b_re_t, b_im_t)


def _s5_in_kernel(x_ref, g_ref, w_ref, u_ref):
    h = _rmsnorm(x_ref[...], g_ref[...]).astype(BF16)
    u_ref[...] = _dot(h, w_ref[...]).astype(BF16)


def _s5_in(x, pre_gain, w_in):
    row_spec = pl.BlockSpec((ROW_TILE, D_MODEL), lambda i: (i, 0))
    return pl.pallas_call(
        _s5_in_kernel,
        grid=(N_TOK // ROW_TILE,),
        in_specs=[row_spec, _const_spec((1, D_MODEL)), _const_spec((D_MODEL, D_MODEL))],
        out_specs=row_spec,
        out_shape=jax.ShapeDtypeStruct((N_TOK, D_MODEL), BF16),
        compiler_params=_params("arbitrary"),
        name="s5_in",
    )(x, pre_gain, w_in)


def _s5_scan_kernel(u_ref, bre_ref, bim_ref, cre_ref, cim_ref, are_ref, aim_ref, d_ref,
                    y_ref, xr_ref, xi_ref, sr_ref, si_ref):
    ti = pl.program_id(1)

    @pl.when(ti == 0)
    def _():
        sr_ref[...] = jnp.zeros_like(sr_ref)
        si_ref[...] = jnp.zeros_like(si_ref)

    u = u_ref[...]
    xr_ref[...] = _dot(u, bre_ref[...])
    xi_ref[...] = _dot(u, bim_ref[...])
    a_re = are_ref[...]
    a_im = aim_ref[...]

    def step(t, carry):
        s_re, s_im = carry
        r0 = pl.multiple_of(t * BATCH, BATCH)
        n_re = a_re * s_re - a_im * s_im + xr_ref[pl.ds(r0, BATCH), :]
        n_im = a_re * s_im + a_im * s_re + xi_ref[pl.ds(r0, BATCH), :]
        xr_ref[pl.ds(r0, BATCH), :] = n_re
        xi_ref[pl.ds(r0, BATCH), :] = n_im
        return n_re, n_im

    s_re, s_im = lax.fori_loop(0, SCAN_STEPS, step, (sr_ref[...], si_ref[...]))
    sr_ref[...] = s_re
    si_ref[...] = s_im

    y = (_dot(xr_ref[...].astype(BF16), cre_ref[...])
         - _dot(xi_ref[...].astype(BF16), cim_ref[...]))
    y = y + d_ref[...] * u.astype(F32)
    y = 0.5 * y * (1.0 + jnp.tanh(math.sqrt(2.0 / math.pi) * (y + 0.044715 * (y * y * y))))
    y_ref[...] = y.astype(BF16)


def _s5_scan(u, b_re, b_im, c_re, c_im, a_re, a_im, d_skip):
    tile = pl.BlockSpec((SCAN_ROWS, GB_CH), lambda g, t: (t, g))
    b_spec = pl.BlockSpec((None, GB_CH, GB_ST), lambda g, t: (g, 0, 0))
    c_spec = pl.BlockSpec((None, GB_ST, GB_CH), lambda g, t: (g, 0, 0))
    a_spec = pl.BlockSpec((None, 1, GB_ST), lambda g, t: (g, 0, 0))
    return pl.pallas_call(
        _s5_scan_kernel,
        grid=(N_GROUP_BLOCKS, N_TOK // SCAN_ROWS),
        in_specs=[tile, b_spec, b_spec, c_spec, c_spec, a_spec, a_spec,
                  pl.BlockSpec((1, GB_CH), lambda g, t: (0, g))],
        out_specs=tile,
        out_shape=jax.ShapeDtypeStruct((N_TOK, D_MODEL), BF16),
        scratch_shapes=[pltpu.VMEM((SCAN_ROWS, GB_ST), F32),
                        pltpu.VMEM((SCAN_ROWS, GB_ST), F32),
                        pltpu.VMEM((BATCH, GB_ST), F32),
                        pltpu.VMEM((BATCH, GB_ST), F32)],
        compiler_params=_params("arbitrary", "arbitrary"),
        name="s5_scan",
    )(u, b_re, b_im, c_re, c_im, a_re, a_im, d_skip)


def _s5_out_kernel(y_ref, x_ref, wglu_ref, wout_ref, g_ref, o_ref):
    y = y_ref[...]
    gated = (y.astype(F32) * _sigmoid(_dot(y, wglu_ref[...]))).astype(BF16)
    o_ref[...] = x_ref[...] + _rmsnorm(_dot(gated, wout_ref[...]), g_ref[...])


def _s5_out(y, x, w_glu, w_out, post_gain):
    row_spec = pl.BlockSpec((ROW_TILE, D_MODEL), lambda i: (i, 0))
    return pl.pallas_call(
        _s5_out_kernel,
        grid=(N_TOK // ROW_TILE,),
        in_specs=[row_spec, row_spec, _const_spec((D_MODEL, D_MODEL)),
                  _const_spec((D_MODEL, D_MODEL)), _const_spec((1, D_MODEL))],
        out_specs=row_spec,
        out_shape=jax.ShapeDtypeStruct((N_TOK, D_MODEL), F32),
        compiler_params=_params("arbitrary"),
        name="s5_out",
    )(y, x, w_glu, w_out, post_gain)


def _block_diag_in(bb_t):
    x = bb_t.reshape(N_GROUP_BLOCKS, GROUP_BLOCK, SSM_GROUP, STATE)
    eye = jnp.eye(GROUP_BLOCK, dtype=bb_t.dtype)
    x = jnp.einsum('klcp,lm->klcmp', x, eye)
    return x.reshape(N_GROUP_BLOCKS, GB_CH, GB_ST).astype(BF16)


def _block_diag_out(c):
    x = c.reshape(N_GROUP_BLOCKS, GROUP_BLOCK, SSM_GROUP, STATE)
    eye = jnp.eye(GROUP_BLOCK, dtype=c.dtype)
    x = jnp.einsum('klcp,lm->klpmc', x, eye)
    return x.reshape(N_GROUP_BLOCKS, GB_ST, GB_CH).astype(BF16)


def kernel(x, fox_w_in, fox_b_f, fox_q_gain, fox_k_gain, fox_w_out, s5_w_in, s5_log_dt, s5_lam_re, s5_lam_im, s5_b_re, s5_b_im, s5_c_re, s5_c_im, s5_d, s5_w_glu, s5_w_out, mix_pre_gain, mix_post_gain, ffn_pre_gain, ffn_post_gain, ffn_w_gate, ffn_w_up, ffn_w_down):
    row = lambda a: a.reshape(1, -1).astype(F32)
    x0 = x.reshape(N_TOK, D_MODEL)

    w_in = fox_w_in[0]
    w_main = w_in[:, :4 * D_ATT].astype(BF16)
    w_f = jnp.pad(w_in[:, 4 * D_ATT:], ((0, 0), (0, LANES - N_HEADS))).astype(BF16)
    b_f = jnp.pad(fox_b_f[0], (0, LANES - N_HEADS)).reshape(1, LANES).astype(F32)
    reps = 256 // HEAD_DIM
    q_gain = row(jnp.tile(fox_q_gain[0] * HEAD_DIM ** -0.5, reps))
    k_gain = row(jnp.tile(fox_k_gain[0], reps))
    head_of = jnp.arange(256) // HEAD_DIM
    e_blk = (head_of[:, None] == head_of[None, :]).astype(BF16)

    q, k, v, sg, c = _fox_in(x0, row(mix_pre_gain[0]), w_main, w_f, b_f, q_gain, k_gain, e_blk)
    ck = c[:, :N_HEADS].reshape(BATCH, SEQ, HEAD_PAIRS, 2).transpose(0, 2, 3, 1)
    o = _attention(q, k, v, ck)
    x1 = _fox_out(o, sg, x0, fox_w_out[0].astype(BF16), row(mix_post_gain[0]))
    x2 = _ffn(x1, row(ffn_pre_gain[0]), ffn_w_gate[0].astype(BF16), ffn_w_up[0].astype(BF16),
              ffn_w_down[0].astype(BF16), row(ffn_post_gain[0]), to_time_major=True)
    x2 = x2.reshape(N_TOK, D_MODEL)

    gps = (N_GROUPS, 1, STATE)
    a_re, a_im, bb_re, bb_im = _s5_discretise(
        s5_log_dt[0].reshape(N_GROUPS, 1, 1).astype(F32),
        s5_lam_re[0].reshape(gps).astype(F32), s5_lam_im[0].reshape(gps).astype(F32),
        s5_b_re[0].transpose(0, 2, 1).astype(F32), s5_b_im[0].transpose(0, 2, 1).astype(F32))
    u = _s5_in(x2, row(mix_pre_gain[1]), s5_w_in[0].astype(BF16))
    y = _s5_scan(u, _block_diag_in(bb_re), _block_diag_in(bb_im),
                 _block_diag_out(s5_c_re[0]), _block_diag_out(s5_c_im[0]),
                 a_re.reshape(N_GROUP_BLOCKS, 1, GB_ST), a_im.reshape(N_GROUP_BLOCKS, 1, GB_ST),
                 row(s5_d[0]))
    x3 = _s5_out(y, x2, s5_w_glu[0].astype(BF16), s5_w_out[0].astype(BF16), row(mix_post_gain[1]))
    x4 = _ffn(x3, row(ffn_pre_gain[1]), ffn_w_gate[1].astype(BF16), ffn_w_up[1].astype(BF16),
              ffn_w_down[1].astype(BF16), row(ffn_post_gain[1]), to_time_major=False)
    return x4.reshape(BATCH, SEQ, D_MODEL)
```

```python
import functools
import math

import jax
import jax.numpy as jnp
from jax import lax
from jax.experimental import pallas as pl
from jax.experimental.pallas import tpu as pltpu

F32 = jnp.float32
BF16 = jnp.bfloat16

D_MODEL = 1024
BATCH = 16
SEQ = 2048
N_TOK = BATCH * SEQ
N_HEADS = 16
HEAD_DIM = 64
D_ATT = N_HEADS * HEAD_DIM
SSM_GROUP = 16
N_GROUPS = 64
STATE = 64
D_FF = 2816
EPS = 1e-6

LANES = 128
ROW_TILE = 512
TILES_PER_SEQ = SEQ // ROW_TILE
ATT_TQ = 256
LOG2E = math.log2(math.e)
HEAD_PAIRS = N_HEADS // 2
SCAN_STEPS = 32
SCAN_ROWS = SCAN_STEPS * BATCH
GROUP_BLOCK = 16
N_GROUP_BLOCKS = N_GROUPS // GROUP_BLOCK
GB_CH = GROUP_BLOCK * SSM_GROUP
GB_ST = GROUP_BLOCK * STATE
FF_CHUNK = D_FF // 2
VMEM_LIMIT = 56 * 1024 * 1024


def _const_spec(shape):
    nd = len(shape)
    return pl.BlockSpec(shape, lambda *_: (0,) * nd, pipeline_mode=pl.Buffered(1))


def _rmsnorm(xf, gain):
    return xf * lax.rsqrt(jnp.mean(xf * xf, axis=-1, keepdims=True) + EPS) * gain


def _dot(a, b):
    return jnp.dot(a, b, preferred_element_type=F32)


def _sigmoid(x):
    return 1.0 / (1.0 + jnp.exp(-x))


def _params(*sem):
    return pltpu.CompilerParams(dimension_semantics=sem, vmem_limit_bytes=VMEM_LIMIT)


def _fox_in_kernel(x_ref, g_ref, w_ref, wf_ref, bf_ref, qg_ref, kg_ref, e_ref,
                   q_ref, k_ref, v_ref, sg_ref, c_ref, carry_ref):
    i = pl.program_id(0)
    h = _rmsnorm(x_ref[...], g_ref[...]).astype(BF16)

    for dst, col0, gain in ((q_ref, 0, qg_ref), (k_ref, D_ATT, kg_ref)):
        for cg in range(D_ATT // 256):
            lo = cg * 256
            blk = _dot(h, w_ref[:, col0 + lo:col0 + lo + 256])
            ssq = _dot((blk * blk).astype(BF16), e_ref[...])
            dst[:, lo:lo + 256] = (
                blk * lax.rsqrt(ssq * (1.0 / HEAD_DIM) + EPS) * gain[...]).astype(BF16)

    v_ref[...] = _dot(h, w_ref[:, 2 * D_ATT:3 * D_ATT]).astype(BF16)
    sg_ref[...] = _sigmoid(_dot(h, w_ref[:, 3 * D_ATT:4 * D_ATT])).astype(BF16)

    z = _dot(h, wf_ref[...]) + bf_ref[...]
    log_f = jnp.minimum(z, 0.0) - jnp.log(1.0 + jnp.exp(-jnp.abs(z)))
    rows = lax.broadcasted_iota(jnp.int32, (ROW_TILE, ROW_TILE), 0)
    cols = lax.broadcasted_iota(jnp.int32, (ROW_TILE, ROW_TILE), 1)
    tril = (rows >= cols).astype(BF16)
    hi = log_f.astype(BF16)
    r1 = log_f - hi.astype(F32)
    mid = r1.astype(BF16)
    lo3 = (r1 - mid.astype(F32)).astype(BF16)
    cum = _dot(tril, hi) + _dot(tril, mid) + _dot(tril, lo3)

    @pl.when(i % TILES_PER_SEQ == 0)
    def _():
        carry_ref[...] = jnp.zeros_like(carry_ref)

    cum = cum + carry_ref[...]
    c_ref[...] = cum * LOG2E
    carry_ref[...] = cum[ROW_TILE - 1:ROW_TILE, :]


def _fox_in(x, pre_gain, w_main, w_f, b_f, q_gain, k_gain, e_blk):
    row_spec = pl.BlockSpec((ROW_TILE, D_MODEL), lambda i: (i, 0))
    act = jax.ShapeDtypeStruct((N_TOK, D_ATT), BF16)
    return pl.pallas_call(
        _fox_in_kernel,
        grid=(N_TOK // ROW_TILE,),
        in_specs=[
            row_spec,
            _const_spec((1, D_MODEL)),
            _const_spec((D_MODEL, 4 * D_ATT)),
            _const_spec((D_MODEL, LANES)),
            _const_spec((1, LANES)),
            _const_spec((1, 256)),
            _const_spec((1, 256)),
            _const_spec((256, 256)),
        ],
        out_specs=[row_spec, row_spec, row_spec, row_spec,
                   pl.BlockSpec((ROW_TILE, LANES), lambda i: (i, 0))],
        out_shape=[act, act, act, act, jax.ShapeDtypeStruct((N_TOK, LANES), F32)],
        scratch_shapes=[pltpu.VMEM((1, LANES), F32)],
        compiler_params=_params("arbitrary"),
        name="fox_in",
    )(x, pre_gain, w_main, w_f, b_f, q_gain, k_gain, e_blk)


def _attn_kernel(q_ref, k_ref, v_ref, ck_ref, o_ref):
    lane = lax.broadcasted_iota(jnp.int32, (SEQ, LANES), 1)
    low_half = lane < HEAD_DIM
    q = q_ref[...].astype(F32)
    v = v_ref[...].astype(F32)
    k = k_ref[...]
    qm = (jnp.where(low_half, q, 0.0).astype(BF16), jnp.where(low_half, 0.0, q).astype(BF16))
    vm = (jnp.where(low_half, v, 1.0).astype(BF16), jnp.where(low_half, 1.0, v).astype(BF16))
    row = lax.broadcasted_iota(jnp.int32, (ATT_TQ, ATT_TQ), 0)
    col = lax.broadcasted_iota(jnp.int32, (ATT_TQ, ATT_TQ), 1)
    causal = row >= col
    low_q = lax.broadcasted_iota(jnp.int32, (ATT_TQ, LANES), 1) < HEAD_DIM
    nt = (((1,), (1,)), ((), ()))

    for qi in range(SEQ // ATT_TQ):
        k0 = qi * ATT_TQ
        outs = []
        for hh in range(2):
            qb = qm[hh][k0:k0 + ATT_TQ]
            s_d = lax.dot_general(qb, k[k0:k0 + ATT_TQ], nt, preferred_element_type=F32)
            s_d = jnp.where(causal, s_d - ck_ref[hh:hh + 1, k0:k0 + ATT_TQ], -1e30)
            m = jnp.max(s_d, axis=-1, keepdims=True)
            if qi > 0:
                s_o = lax.dot_general(qb, k[:k0], nt, preferred_element_type=F32)
                s_o = s_o - ck_ref[hh:hh + 1, :k0]
                m = jnp.maximum(m, jnp.max(s_o, axis=-1, keepdims=True))
            acc = _dot(jnp.exp2(s_d - m).astype(BF16), vm[hh][k0:k0 + ATT_TQ])
            if qi > 0:
                acc = acc + _dot(jnp.exp2(s_o - m).astype(BF16), vm[hh][:k0])
            outs.append(acc / pltpu.roll(acc, HEAD_DIM, 1))
        o_ref[k0:k0 + ATT_TQ, :] = jnp.where(low_q, outs[0], outs[1]).astype(BF16)


def _attention(q, k, v, ck):
    seq_spec = pl.BlockSpec((SEQ, LANES), lambda b, j: (b, j))
    return pl.pallas_call(
        _attn_kernel,
        grid=(BATCH, HEAD_PAIRS),
        in_specs=[seq_spec, seq_spec, seq_spec,
                  pl.BlockSpec((None, None, 2, SEQ), lambda b, j: (b, j, 0, 0))],
        out_specs=seq_spec,
        out_shape=jax.ShapeDtypeStruct((N_TOK, D_ATT), BF16),
        compiler_params=_params("arbitrary", "arbitrary"),
        name="fox_attention",
    )(q, k, v, ck)


def _fox_out_kernel(o_ref, sg_ref, x_ref, w_ref, g_ref, y_ref):
    m = _dot(o_ref[...] * sg_ref[...], w_ref[...])
    y_ref[...] = x_ref[...] + _rmsnorm(m, g_ref[...])


def _fox_out(o, sg, x, w_out, post_gain):
    row_spec = pl.BlockSpec((ROW_TILE, D_MODEL), lambda i: (i, 0))
    return pl.pallas_call(
        _fox_out_kernel,
        grid=(N_TOK // ROW_TILE,),
        in_specs=[row_spec, row_spec, row_spec,
                  _const_spec((D_ATT, D_MODEL)), _const_spec((1, D_MODEL))],
        out_specs=row_spec,
        out_shape=jax.ShapeDtypeStruct((N_TOK, D_MODEL), F32),
        compiler_params=_params("arbitrary"),
        name="fox_out",
    )(o, sg, x, w_out, post_gain)


def _ffn_kernel(x_ref, pre_ref, wg_ref, wu_ref, wd_ref, post_ref, y_ref):
    x = x_ref[...]
    h = _rmsnorm(x, pre_ref[...]).astype(BF16)
    acc = None
    for c0 in range(0, D_FF, FF_CHUNK):
        g = _dot(h, wg_ref[:, c0:c0 + FF_CHUNK])
        u = _dot(h, wu_ref[:, c0:c0 + FF_CHUNK])
        a = (g * _sigmoid(g) * u).astype(BF16)
        part = _dot(a, wd_ref[c0:c0 + FF_CHUNK, :])
        acc = part if acc is None else acc + part
    y_ref[...] = x + _rmsnorm(acc, post_ref[...])


def _batch_major_spec():
    return pl.BlockSpec((ROW_TILE, D_MODEL), lambda i: (i, 0))


def _time_major_spec():
    return pl.BlockSpec((ROW_TILE, D_MODEL),
                        lambda i: (i % TILES_PER_SEQ, i // TILES_PER_SEQ))


def _ffn(x, pre_gain, w_gate, w_up, w_down, post_gain, *, to_time_major):
    if to_time_major:
        in_spec, out_spec = _batch_major_spec(), _time_major_spec()
        out_shape = (SEQ, BATCH * D_MODEL)
    else:
        in_spec, out_spec = _time_major_spec(), _batch_major_spec()
        out_shape = (N_TOK, D_MODEL)
        x = x.reshape(SEQ, BATCH * D_MODEL)
    return pl.pallas_call(
        _ffn_kernel,
        grid=(N_TOK // ROW_TILE,),
        in_specs=[in_spec, _const_spec((1, D_MODEL)),
                  _const_spec((D_MODEL, D_FF)), _const_spec((D_MODEL, D_FF)),
                  _const_spec((D_FF, D_MODEL)), _const_spec((1, D_MODEL))],
        out_specs=out_spec,
        out_shape=jax.ShapeDtypeStruct(out_shape, F32),
        compiler_params=_params("arbitrary"),
        name="ffn_to_time_major" if to_time_major else "ffn_to_batch_major",
    )(x, pre_gain, w_gate, w_up, w_down, post_gain)


def _s5_disc_kernel(ldt_ref, lre_ref, lim_ref, bre_ref, bim_ref,
                    are_ref, aim_ref, bbre_ref, bbim_ref):
    dt = jnp.exp(ldt_ref[...])
    lam_re = lre_ref[...]
    lam_im = lim_ref[...]
    mag = jnp.exp(lam_re * dt)
    a_re = mag * jnp.cos(lam_im * dt)
    a_im = mag * jnp.sin(lam_im * dt)
    den = lam_re * lam_re + lam_im * lam_im
    n_re = a_re - 1.0
    z_re = (n_re * lam_re + a_im * lam_im) / den
    z_im = (a_im * lam_re - n_re * lam_im) / den
    are_ref[...] = a_re
    aim_ref[...] = a_im
    b_re = bre_ref[...]
    b_im = bim_ref[...]
    bbre_ref[...] = z_re * b_re - z_im * b_im
    bbim_ref[...] = z_re * b_im + z_im * b_re


def _s5_discretise(log_dt, lam_re, lam_im, b_re_t, b_im_t):
    gs = jax.ShapeDtypeStruct((N_GROUPS, 1, STATE), F32)
    gcs = jax.ShapeDtypeStruct((N_GROUPS, SSM_GROUP, STATE), F32)
    return pl.pallas_call(
        _s5_disc_kernel,
        out_shape=[gs, gs, gcs, gcs],
        name="s5_discretise",
    )(log_dt, lam_re, lam_im, b_re_t, b_im_t)


def _s5_in_kernel(x_ref, g_ref, w_ref, u_ref):
    h = _rmsnorm(x_ref[...], g_ref[...]).astype(BF16)
    u_ref[...] = _dot(h, w_ref[...]).astype(BF16)


def _s5_in(x, pre_gain, w_in):
    row_spec = pl.BlockSpec((ROW_TILE, D_MODEL), lambda i: (i, 0))
    return pl.pallas_call(
        _s5_in_kernel,
        grid=(N_TOK // ROW_TILE,),
        in_specs=[row_spec, _const_spec((1, D_MODEL)), _const_spec((D_MODEL, D_MODEL))],
        out_specs=row_spec,
        out_shape=jax.ShapeDtypeStruct((N_TOK, D_MODEL), BF16),
        compiler_params=_params("arbitrary"),
        name="s5_in",
    )(x, pre_gain, w_in)


def _s5_scan_kernel(u_ref, bre_ref, bim_ref, cre_ref, cim_ref, are_ref, aim_ref, d_ref,
                    y_ref, xr_ref, xi_ref, sr_ref, si_ref):
    ti = pl.program_id(1)

    @pl.when(ti == 0)
    def _():
        sr_ref[...] = jnp.zeros_like(sr_ref)
        si_ref[...] = jnp.zeros_like(si_ref)

    u = u_ref[...]
    xr_ref[...] = _dot(u, bre_ref[...])
    xi_ref[...] = _dot(u, bim_ref[...])
    a_re = are_ref[...]
    a_im = aim_ref[...]

    def step(t, carry):
        s_re, s_im = carry
        r0 = pl.multiple_of(t * BATCH, BATCH)
        n_re = a_re * s_re - a_im * s_im + xr_ref[pl.ds(r0, BATCH), :]
        n_im = a_re * s_im + a_im * s_re + xi_ref[pl.ds(r0, BATCH), :]
        xr_ref[pl.ds(r0, BATCH), :] = n_re
        xi_ref[pl.ds(r0, BATCH), :] = n_im
        return n_re, n_im

    s_re, s_im = lax.fori_loop(0, SCAN_STEPS, step, (sr_ref[...], si_ref[...]))
    sr_ref[...] = s_re
    si_ref[...] = s_im

    y = (_dot(xr_ref[...].astype(BF16), cre_ref[...])
         - _dot(xi_ref[...].astype(BF16), cim_ref[...]))
    y = y + d_ref[...] * u.astype(F32)
    y = 0.5 * y * (1.0 + jnp.tanh(math.sqrt(2.0 / math.pi) * (y + 0.044715 * (y * y * y))))
    y_ref[...] = y.astype(BF16)


def _s5_scan(u, b_re, b_im, c_re, c_im, a_re, a_im, d_skip):
    tile = pl.BlockSpec((SCAN_ROWS, GB_CH), lambda g, t: (t, g))
    b_spec = pl.BlockSpec((None, GB_CH, GB_ST), lambda g, t: (g, 0, 0))
    c_spec = pl.BlockSpec((None, GB_ST, GB_CH), lambda g, t: (g, 0, 0))
    a_spec = pl.BlockSpec((None, 1, GB_ST), lambda g, t: (g, 0, 0))
    return pl.pallas_call(
        _s5_scan_kernel,
        grid=(N_GROUP_BLOCKS, N_TOK // SCAN_ROWS),
        in_specs=[tile, b_spec, b_spec, c_spec, c_spec, a_spec, a_spec,
                  pl.BlockSpec((1, GB_CH), lambda g, t: (0, g))],
        out_specs=tile,
        out_shape=jax.ShapeDtypeStruct((N_TOK, D_MODEL), BF16),
        scratch_shapes=[pltpu.VMEM((SCAN_ROWS, GB_ST), F32),
                        pltpu.VMEM((SCAN_ROWS, GB_ST), F32),
                        pltpu.VMEM((BATCH, GB_ST), F32),
                        pltpu.VMEM((BATCH, GB_ST), F32)],
        compiler_params=_params("arbitrary", "arbitrary"),
        name="s5_scan",
    )(u, b_re, b_im, c_re, c_im, a_re, a_im, d_skip)


def _s5_out_kernel(y_ref, x_ref, wglu_ref, wout_ref, g_ref, o_ref):
    y = y_ref[...]
    gated = (y.astype(F32) * _sigmoid(_dot(y, wglu_ref[...]))).astype(BF16)
    o_ref[...] = x_ref[...] + _rmsnorm(_dot(gated, wout_ref[...]), g_ref[...])


def _s5_out(y, x, w_glu, w_out, post_gain):
    row_spec = pl.BlockSpec((ROW_TILE, D_MODEL), lambda i: (i, 0))
    return pl.pallas_call(
        _s5_out_kernel,
        grid=(N_TOK // ROW_TILE,),
        in_specs=[row_spec, row_spec, _const_spec((D_MODEL, D_MODEL)),
                  _const_spec((D_MODEL, D_MODEL)), _const_spec((1, D_MODEL))],
        out_specs=row_spec,
        out_shape=jax.ShapeDtypeStruct((N_TOK, D_MODEL), F32),
        compiler_params=_params("arbitrary"),
        name="s5_out",
    )(y, x, w_glu, w_out, post_gain)


def _block_diag_in(bb_t):
    x = bb_t.reshape(N_GROUP_BLOCKS, GROUP_BLOCK, SSM_GROUP, STATE)
    eye = jnp.eye(GROUP_BLOCK, dtype=bb_t.dtype)
    x = jnp.einsum('klcp,lm->klcmp', x, eye)
    return x.reshape(N_GROUP_BLOCKS, GB_CH, GB_ST).astype(BF16)


def _block_diag_out(c):
    x = c.reshape(N_GROUP_BLOCKS, GROUP_BLOCK, SSM_GROUP, STATE)
    eye = jnp.eye(GROUP_BLOCK, dtype=c.dtype)
    x = jnp.einsum('klcp,lm->klpmc', x, eye)
    return x.reshape(N_GROUP_BLOCKS, GB_ST, GB_CH).astype(BF16)


def kernel(x, fox_w_in, fox_b_f, fox_q_gain, fox_k_gain, fox_w_out, s5_w_in, s5_log_dt, s5_lam_re, s5_lam_im, s5_b_re, s5_b_im, s5_c_re, s5_c_im, s5_d, s5_w_glu, s5_w_out, mix_pre_gain, mix_post_gain, ffn_pre_gain, ffn_post_gain, ffn_w_gate, ffn_w_up, ffn_w_down):
    row = lambda a: a.reshape(1, -1).astype(F32)
    x0 = x.reshape(N_TOK, D_MODEL)

    w_in = fox_w_in[0]
    w_main = w_in[:, :4 * D_ATT].astype(BF16)
    w_f = jnp.pad(w_in[:, 4 * D_ATT:], ((0, 0), (0, LANES - N_HEADS))).astype(BF16)
    b_f = jnp.pad(fox_b_f[0], (0, LANES - N_HEADS)).reshape(1, LANES).astype(F32)
    reps = 256 // HEAD_DIM
    q_gain = row(jnp.tile(fox_q_gain[0] * (HEAD_DIM ** -0.5 * LOG2E), reps))
    k_gain = row(jnp.tile(fox_k_gain[0], reps))
    head_of = jnp.arange(256) // HEAD_DIM
    e_blk = (head_of[:, None] == head_of[None, :]).astype(BF16)

    q, k, v, sg, c = _fox_in(x0, row(mix_pre_gain[0]), w_main, w_f, b_f, q_gain, k_gain, e_blk)
    ck = c[:, :N_HEADS].reshape(BATCH, SEQ, HEAD_PAIRS, 2).transpose(0, 2, 3, 1)
    o = _attention(q, k, v, ck)
    x1 = _fox_out(o, sg, x0, fox_w_out[0].astype(BF16), row(mix_post_gain[0]))
    x2 = _ffn(x1, row(ffn_pre_gain[0]), ffn_w_gate[0].astype(BF16), ffn_w_up[0].astype(BF16),
              ffn_w_down[0].astype(BF16), row(ffn_post_gain[0]), to_time_major=True)
    x2 = x2.reshape(N_TOK, D_MODEL)

    gps = (N_GROUPS, 1, STATE)
    a_re, a_im, bb_re, bb_im = _s5_discretise(
        s5_log_dt[0].reshape(N_GROUPS, 1, 1).astype(F32),
        s5_lam_re[0].reshape(gps).astype(F32), s5_lam_im[0].reshape(gps).astype(F32),
        s5_b_re[0].transpose(0, 2, 1).astype(F32), s5_b_im[0].transpose(0, 2, 1).astype(F32))
    u = _s5_in(x2, row(mix_pre_gain[1]), s5_w_in[0].astype(BF16))
    y = _s5_scan(u, _block_diag_in(bb_re), _block_diag_in(bb_im),
                 _block_diag_out(s5_c_re[0]), _block_diag_out(s5_c_im[0]),
                 a_re.reshape(N_GROUP_BLOCKS, 1, GB_ST), a_im.reshape(N_GROUP_BLOCKS, 1, GB_ST),
                 row(s5_d[0]))
    x3 = _s5_out(y, x2, s5_w_glu[0].astype(BF16), s5_w_out[0].astype(BF16), row(mix_post_gain[1]))
    x4 = _ffn(x3, row(ffn_pre_gain[1]), ffn_w_gate[1].astype(BF16), ffn_w_up[1].astype(BF16),
              ffn_w_down[1].astype(BF16), row(ffn_post_gain[1]), to_time_major=False)
    return x4.reshape(BATCH, SEQ, D_MODEL)
```

```python
import math

import numpy as np

import jax
import jax.numpy as jnp
from jax import lax
from jax.experimental import pallas as pl
from jax.experimental.pallas import tpu as pltpu

F32 = jnp.float32
BF16 = jnp.bfloat16

D_MODEL = 1024
BATCH = 16
SEQ = 2048
N_TOK = BATCH * SEQ
N_HEADS = 16
HEAD_DIM = 64
D_ATT = N_HEADS * HEAD_DIM
SSM_GROUP = 16
N_GROUPS = 64
STATE = 64
D_FF = 2816
EPS = 1e-6

LANES = 128
ROW_TILE = 512
TILES_PER_SEQ = SEQ // ROW_TILE
ATT_TQ = 256
LOG2E = math.log2(math.e)
HEAD_PAIRS = N_HEADS // 2
SCAN_STEPS = 32
SCAN_ROWS = SCAN_STEPS * BATCH
GROUP_BLOCK = 16
N_GROUP_BLOCKS = N_GROUPS // GROUP_BLOCK
GB_CH = GROUP_BLOCK * SSM_GROUP
GB_ST = GROUP_BLOCK * STATE
FF_CHUNK = D_FF // 2
VMEM_LIMIT = 56 * 1024 * 1024


def _const_spec(shape):
    nd = len(shape)
    return pl.BlockSpec(shape, lambda *_: (0,) * nd, pipeline_mode=pl.Buffered(1))


def _rmsnorm(xf, gain):
    return xf * lax.rsqrt(jnp.mean(xf * xf, axis=-1, keepdims=True) + EPS) * gain


def _dot(a, b):
    return jnp.dot(a, b, preferred_element_type=F32)


def _sigmoid(x):
    return 1.0 / (1.0 + jnp.exp(-x))


def _params(*sem):
    return pltpu.CompilerParams(dimension_semantics=sem, vmem_limit_bytes=VMEM_LIMIT)


def _fox_in_kernel(x_ref, g_ref, w_ref, wf_ref, bf_ref, qg_ref, kg_ref, e_ref,
                   q_ref, k_ref, v_ref, sg_ref, c_ref, carry_ref):
    i = pl.program_id(0)
    h = _rmsnorm(x_ref[...], g_ref[...]).astype(BF16)

    for dst, col0, gain in ((q_ref, 0, qg_ref), (k_ref, D_ATT, kg_ref)):
        for cg in range(D_ATT // 256):
            lo = cg * 256
            blk = _dot(h, w_ref[:, col0 + lo:col0 + lo + 256])
            ssq = _dot((blk * blk).astype(BF16), e_ref[...])
            dst[:, lo:lo + 256] = (
                blk * lax.rsqrt(ssq * (1.0 / HEAD_DIM) + EPS) * gain[...]).astype(BF16)

    v_ref[...] = _dot(h, w_ref[:, 2 * D_ATT:3 * D_ATT]).astype(BF16)
    sg_ref[...] = _sigmoid(_dot(h, w_ref[:, 3 * D_ATT:4 * D_ATT])).astype(BF16)

    z = _dot(h, wf_ref[...]) + bf_ref[...]
    log_f = jnp.minimum(z, 0.0) - jnp.log(1.0 + jnp.exp(-jnp.abs(z)))
    rows = lax.broadcasted_iota(jnp.int32, (ROW_TILE, ROW_TILE), 0)
    cols = lax.broadcasted_iota(jnp.int32, (ROW_TILE, ROW_TILE), 1)
    tril = (rows >= cols).astype(BF16)
    hi = log_f.astype(BF16)
    r1 = log_f - hi.astype(F32)
    mid = r1.astype(BF16)
    lo3 = (r1 - mid.astype(F32)).astype(BF16)
    cum = _dot(tril, hi) + _dot(tril, mid) + _dot(tril, lo3)

    @pl.when(i % TILES_PER_SEQ == 0)
    def _():
        carry_ref[...] = jnp.zeros_like(carry_ref)

    cum = cum + carry_ref[...]
    c_ref[...] = cum * LOG2E
    carry_ref[...] = cum[ROW_TILE - 1:ROW_TILE, :]


def _fox_in(x, pre_gain, w_main, w_f, b_f, q_gain, k_gain, e_blk):
    row_spec = pl.BlockSpec((ROW_TILE, D_MODEL), lambda i: (i, 0))
    act = jax.ShapeDtypeStruct((N_TOK, D_ATT), BF16)
    return pl.pallas_call(
        _fox_in_kernel,
        grid=(N_TOK // ROW_TILE,),
        in_specs=[
            row_spec,
            _const_spec((1, D_MODEL)),
            _const_spec((D_MODEL, 4 * D_ATT)),
            _const_spec((D_MODEL, LANES)),
            _const_spec((1, LANES)),
            _const_spec((1, 256)),
            _const_spec((1, 256)),
            _const_spec((256, 256)),
        ],
        out_specs=[row_spec, row_spec, row_spec, row_spec,
                   pl.BlockSpec((ROW_TILE, LANES), lambda i: (i, 0))],
        out_shape=[act, act, act, act, jax.ShapeDtypeStruct((N_TOK, LANES), F32)],
        scratch_shapes=[pltpu.VMEM((1, LANES), F32)],
        compiler_params=_params("arbitrary"),
        name="fox_in",
    )(x, pre_gain, w_main, w_f, b_f, q_gain, k_gain, e_blk)


def _attn_kernel(q_ref, k_ref, v_ref, ck_ref, o_ref):
    lane = lax.broadcasted_iota(jnp.int32, (SEQ, LANES), 1)
    low_half = lane < HEAD_DIM
    q = q_ref[...].astype(F32)
    v = v_ref[...].astype(F32)
    k = k_ref[...]
    qm = (jnp.where(low_half, q, 0.0).astype(BF16), jnp.where(low_half, 0.0, q).astype(BF16))
    vm = (jnp.where(low_half, v, 1.0).astype(BF16), jnp.where(low_half, 1.0, v).astype(BF16))
    row = lax.broadcasted_iota(jnp.int32, (ATT_TQ, ATT_TQ), 0)
    col = lax.broadcasted_iota(jnp.int32, (ATT_TQ, ATT_TQ), 1)
    causal = row >= col
    low_q = lax.broadcasted_iota(jnp.int32, (ATT_TQ, LANES), 1) < HEAD_DIM
    nt = (((1,), (1,)), ((), ()))

    for qi in range(SEQ // ATT_TQ):
        k0 = qi * ATT_TQ
        outs = []
        for hh in range(2):
            qb = qm[hh][k0:k0 + ATT_TQ]
            s_d = lax.dot_general(qb, k[k0:k0 + ATT_TQ], nt, preferred_element_type=F32)
            s_d = jnp.where(causal, s_d - ck_ref[hh:hh + 1, k0:k0 + ATT_TQ], -1e30)
            m = jnp.max(s_d, axis=-1, keepdims=True)
            if qi > 0:
                s_o = lax.dot_general(qb, k[:k0], nt, preferred_element_type=F32)
                s_o = s_o - ck_ref[hh:hh + 1, :k0]
                m = jnp.maximum(m, jnp.max(s_o, axis=-1, keepdims=True))
            acc = _dot(jnp.exp2(s_d - m).astype(BF16), vm[hh][k0:k0 + ATT_TQ])
            if qi > 0:
                acc = acc + _dot(jnp.exp2(s_o - m).astype(BF16), vm[hh][:k0])
            outs.append(acc / pltpu.roll(acc, HEAD_DIM, 1))
        o_ref[k0:k0 + ATT_TQ, :] = jnp.where(low_q, outs[0], outs[1]).astype(BF16)


def _attention(q, k, v, ck):
    seq_spec = pl.BlockSpec((SEQ, LANES), lambda b, j: (b, j))
    return pl.pallas_call(
        _attn_kernel,
        grid=(BATCH, HEAD_PAIRS),
        in_specs=[seq_spec, seq_spec, seq_spec,
                  pl.BlockSpec((None, None, 2, SEQ), lambda b, j: (b, j, 0, 0))],
        out_specs=seq_spec,
        out_shape=jax.ShapeDtypeStruct((N_TOK, D_ATT), BF16),
        compiler_params=_params("arbitrary", "arbitrary"),
        name="fox_attention",
    )(q, k, v, ck)


def _ffn_body(x, pre_ref, wg_ref, wu_ref, wd_ref, post_ref):
    h = _rmsnorm(x, pre_ref[...]).astype(BF16)
    acc = None
    for c0 in range(0, D_FF, FF_CHUNK):
        g = _dot(h, wg_ref[:, c0:c0 + FF_CHUNK])
        u = _dot(h, wu_ref[:, c0:c0 + FF_CHUNK])
        a = (g * _sigmoid(g) * u).astype(BF16)
        part = _dot(a, wd_ref[c0:c0 + FF_CHUNK, :])
        acc = part if acc is None else acc + part
    return x + _rmsnorm(acc, post_ref[...])


def _row_spec():
    return pl.BlockSpec((ROW_TILE, D_MODEL), lambda i: (i, 0))


def _ffn_specs():
    return [_const_spec((1, D_MODEL)), _const_spec((D_MODEL, D_FF)),
            _const_spec((D_MODEL, D_FF)), _const_spec((D_FF, D_MODEL)),
            _const_spec((1, D_MODEL))]


def _layer0_tail_kernel(o_ref, sg_ref, x_ref, wo_ref, mpost_ref,
                        fpre_ref, wg_ref, wu_ref, wd_ref, fpost_ref,
                        spre_ref, wsin_ref, x2_ref, u_ref):
    m = _dot(o_ref[...] * sg_ref[...], wo_ref[...])
    x1 = x_ref[...] + _rmsnorm(m, mpost_ref[...])
    x2 = _ffn_body(x1, fpre_ref, wg_ref, wu_ref, wd_ref, fpost_ref)
    x2_ref[...] = x2
    u_ref[...] = _dot(_rmsnorm(x2, spre_ref[...]).astype(BF16), wsin_ref[...]).astype(BF16)


def _layer0_tail(o, sg, x, w_out, mix_post, ffn_w, s5_pre, s5_w_in):
    return pl.pallas_call(
        _layer0_tail_kernel,
        grid=(N_TOK // ROW_TILE,),
        in_specs=[_row_spec(), _row_spec(), _row_spec(),
                  _const_spec((D_ATT, D_MODEL)), _const_spec((1, D_MODEL)),
                  *_ffn_specs(),
                  _const_spec((1, D_MODEL)), _const_spec((D_MODEL, D_MODEL))],
        out_specs=[_row_spec(), _row_spec()],
        out_shape=[jax.ShapeDtypeStruct((N_TOK, D_MODEL), F32),
                   jax.ShapeDtypeStruct((N_TOK, D_MODEL), BF16)],
        compiler_params=_params("arbitrary"),
        name="layer0_tail",
    )(o, sg, x, w_out, mix_post, *ffn_w, s5_pre, s5_w_in)


def _layer1_tail_kernel(y_ref, x_ref, wglu_ref, wo_ref, mpost_ref,
                        fpre_ref, wg_ref, wu_ref, wd_ref, fpost_ref, out_ref):
    y = y_ref[...]
    gated = (y.astype(F32) * _sigmoid(_dot(y, wglu_ref[...]))).astype(BF16)
    x3 = x_ref[...] + _rmsnorm(_dot(gated, wo_ref[...]), mpost_ref[...])
    out_ref[...] = _ffn_body(x3, fpre_ref, wg_ref, wu_ref, wd_ref, fpost_ref)


def _layer1_tail(y, x, w_glu, w_out, mix_post, ffn_w):
    return pl.pallas_call(
        _layer1_tail_kernel,
        grid=(N_TOK // ROW_TILE,),
        in_specs=[_row_spec(), _row_spec(),
                  _const_spec((D_MODEL, D_MODEL)), _const_spec((D_MODEL, D_MODEL)),
                  _const_spec((1, D_MODEL)), *_ffn_specs()],
        out_specs=_row_spec(),
        out_shape=jax.ShapeDtypeStruct((N_TOK, D_MODEL), F32),
        compiler_params=_params("arbitrary"),
        name="layer1_tail",
    )(y, x, w_glu, w_out, mix_post, *ffn_w)


def _s5_disc_kernel(ldt_ref, lre_ref, lim_ref, bre_ref, bim_ref,
                    are_ref, aim_ref, bbre_ref, bbim_ref):
    dt = jnp.exp(ldt_ref[...])
    lam_re = lre_ref[...]
    lam_im = lim_ref[...]
    mag = jnp.exp(lam_re * dt)
    a_re = mag * jnp.cos(lam_im * dt)
    a_im = mag * jnp.sin(lam_im * dt)
    den = lam_re * lam_re + lam_im * lam_im
    n_re = a_re - 1.0
    z_re = (n_re * lam_re + a_im * lam_im) / den
    z_im = (a_im * lam_re - n_re * lam_im) / den
    are_ref[...] = a_re
    aim_ref[...] = a_im
    b_re = bre_ref[...]
    b_im = bim_ref[...]
    bbre_ref[...] = z_re * b_re - z_im * b_im
    bbim_ref[...] = z_re * b_im + z_im * b_re


def _s5_discretise(log_dt, lam_re, lam_im, b_re_t, b_im_t):
    gs = jax.ShapeDtypeStruct((N_GROUPS, 1, STATE), F32)
    gcs = jax.ShapeDtypeStruct((N_GROUPS, SSM_GROUP, STATE), F32)
    return pl.pallas_call(
        _s5_disc_kernel,
        out_shape=[gs, gs, gcs, gcs],
        name="s5_discretise",
    )(log_dt, lam_re, lam_im, b_re_t, b_im_t)


def _s5_scan_kernel(u_ref, p_ref, pt_ref, bre_ref, bim_ref, cre_ref, cim_ref,
                    are_ref, aim_ref, d_ref, y_ref, xr_ref, xi_ref, sr_ref, si_ref):
    ti = pl.program_id(1)

    @pl.when(ti == 0)
    def _():
        sr_ref[...] = jnp.zeros_like(sr_ref)
        si_ref[...] = jnp.zeros_like(si_ref)

    u_f32 = _dot(p_ref[...], u_ref[...].reshape(SCAN_ROWS, GB_CH))
    u = u_f32.astype(BF16)
    xr_ref[...] = _dot(u, bre_ref[...])
    xi_ref[...] = _dot(u, bim_ref[...])
    a_re = are_ref[...]
    a_im = aim_ref[...]

    def step(t, carry):
        s_re, s_im = carry
        r0 = pl.multiple_of(t * BATCH, BATCH)
        n_re = a_re * s_re - a_im * s_im + xr_ref[pl.ds(r0, BATCH), :]
        n_im = a_re * s_im + a_im * s_re + xi_ref[pl.ds(r0, BATCH), :]
        xr_ref[pl.ds(r0, BATCH), :] = n_re
        xi_ref[pl.ds(r0, BATCH), :] = n_im
        return n_re, n_im

    s_re, s_im = lax.fori_loop(0, SCAN_STEPS, step, (sr_ref[...], si_ref[...]))
    sr_ref[...] = s_re
    si_ref[...] = s_im

    y = (_dot(xr_ref[...].astype(BF16), cre_ref[...])
         - _dot(xi_ref[...].astype(BF16), cim_ref[...]))
    y = y + d_ref[...] * u_f32
    y = 0.5 * y * (1.0 + jnp.tanh(math.sqrt(2.0 / math.pi) * (y + 0.044715 * (y * y * y))))
    y_bm = _dot(pt_ref[...], y.astype(BF16)).astype(BF16)
    y_ref[...] = y_bm.reshape(BATCH, SCAN_STEPS, GB_CH)


def _time_major_permutation():
    r = np.arange(SCAN_ROWS)
    src = (r % BATCH) * SCAN_STEPS + r // BATCH
    return (src[:, None] == np.arange(SCAN_ROWS)[None, :]).astype(np.float32)


def _s5_scan(u, b_re, b_im, c_re, c_im, a_re, a_im, d_skip):
    perm = _time_major_permutation()
    tile = pl.BlockSpec((BATCH, SCAN_STEPS, GB_CH), lambda g, t: (0, t, g))
    b_spec = pl.BlockSpec((None, GB_CH, GB_ST), lambda g, t: (g, 0, 0))
    c_spec = pl.BlockSpec((None, GB_ST, GB_CH), lambda g, t: (g, 0, 0))
    a_spec = pl.BlockSpec((None, 1, GB_ST), lambda g, t: (g, 0, 0))
    p_spec = pl.BlockSpec((SCAN_ROWS, SCAN_ROWS), lambda g, t: (0, 0))
    return pl.pallas_call(
        _s5_scan_kernel,
        grid=(N_GROUP_BLOCKS, SEQ // SCAN_STEPS),
        in_specs=[tile, p_spec, p_spec, b_spec, b_spec, c_spec, c_spec, a_spec, a_spec,
                  pl.BlockSpec((1, GB_CH), lambda g, t: (0, g))],
        out_specs=tile,
        out_shape=jax.ShapeDtypeStruct((BATCH, SEQ, D_MODEL), BF16),
        scratch_shapes=[pltpu.VMEM((SCAN_ROWS, GB_ST), F32),
                        pltpu.VMEM((SCAN_ROWS, GB_ST), F32),
                        pltpu.VMEM((BATCH, GB_ST), F32),
                        pltpu.VMEM((BATCH, GB_ST), F32)],
        compiler_params=_params("arbitrary", "arbitrary"),
        name="s5_scan",
    )(u.reshape(BATCH, SEQ, D_MODEL), jnp.asarray(perm, BF16), jnp.asarray(perm.T, BF16),
      b_re, b_im, c_re, c_im, a_re, a_im, d_skip)


def _block_diag_in(bb_t):
    x = bb_t.reshape(N_GROUP_BLOCKS, GROUP_BLOCK, SSM_GROUP, STATE)
    eye = jnp.eye(GROUP_BLOCK, dtype=bb_t.dtype)
    x = jnp.einsum('klcp,lm->klcmp', x, eye)
    return x.reshape(N_GROUP_BLOCKS, GB_CH, GB_ST).astype(BF16)


def _block_diag_out(c):
    x = c.reshape(N_GROUP_BLOCKS, GROUP_BLOCK, SSM_GROUP, STATE)
    eye = jnp.eye(GROUP_BLOCK, dtype=c.dtype)
    x = jnp.einsum('klcp,lm->klpmc', x, eye)
    return x.reshape(N_GROUP_BLOCKS, GB_ST, GB_CH).astype(BF16)


def kernel(x, fox_w_in, fox_b_f, fox_q_gain, fox_k_gain, fox_w_out, s5_w_in, s5_log_dt, s5_lam_re, s5_lam_im, s5_b_re, s5_b_im, s5_c_re, s5_c_im, s5_d, s5_w_glu, s5_w_out, mix_pre_gain, mix_post_gain, ffn_pre_gain, ffn_post_gain, ffn_w_gate, ffn_w_up, ffn_w_down):
    row = lambda a: a.reshape(1, -1).astype(F32)
    x0 = x.reshape(N_TOK, D_MODEL)

    def ffn_weights(i):
        return (row(ffn_pre_gain[i]), ffn_w_gate[i].astype(BF16), ffn_w_up[i].astype(BF16),
                ffn_w_down[i].astype(BF16), row(ffn_post_gain[i]))

    w_in = fox_w_in[0]
    w_main = w_in[:, :4 * D_ATT].astype(BF16)
    w_f = jnp.pad(w_in[:, 4 * D_ATT:], ((0, 0), (0, LANES - N_HEADS))).astype(BF16)
    b_f = jnp.pad(fox_b_f[0], (0, LANES - N_HEADS)).reshape(1, LANES).astype(F32)
    reps = 256 // HEAD_DIM
    q_gain = row(jnp.tile(fox_q_gain[0] * (HEAD_DIM ** -0.5 * LOG2E), reps))
    k_gain = row(jnp.tile(fox_k_gain[0], reps))
    head_of = jnp.arange(256) // HEAD_DIM
    e_blk = (head_of[:, None] == head_of[None, :]).astype(BF16)

    q, k, v, sg, c = _fox_in(x0, row(mix_pre_gain[0]), w_main, w_f, b_f, q_gain, k_gain, e_blk)
    ck = c[:, :N_HEADS].reshape(BATCH, SEQ, HEAD_PAIRS, 2).transpose(0, 2, 3, 1)
    o = _attention(q, k, v, ck)
    x2, u = _layer0_tail(o, sg, x0, fox_w_out[0].astype(BF16), row(mix_post_gain[0]),
                         ffn_weights(0), row(mix_pre_gain[1]), s5_w_in[0].astype(BF16))

    gps = (N_GROUPS, 1, STATE)
    a_re, a_im, bb_re, bb_im = _s5_discretise(
        s5_log_dt[0].reshape(N_GROUPS, 1, 1).astype(F32),
        s5_lam_re[0].reshape(gps).astype(F32), s5_lam_im[0].reshape(gps).astype(F32),
        s5_b_re[0].transpose(0, 2, 1).astype(F32), s5_b_im[0].transpose(0, 2, 1).astype(F32))
    y = _s5_scan(u, _block_diag_in(bb_re), _block_diag_in(bb_im),
                 _block_diag_out(s5_c_re[0]), _block_diag_out(s5_c_im[0]),
                 a_re.reshape(N_GROUP_BLOCKS, 1, GB_ST), a_im.reshape(N_GROUP_BLOCKS, 1, GB_ST),
                 row(s5_d[0]))
    x4 = _layer1_tail(y.reshape(N_TOK, D_MODEL), x2, s5_w_glu[0].astype(BF16),
                      s5_w_out[0].astype(BF16), row(mix_post_gain[1]), ffn_weights(1))
    return x4.reshape(BATCH, SEQ, D_MODEL)
```

```python
import math

import numpy as np

import jax
import jax.numpy as jnp
from jax import lax
from jax.experimental import pallas as pl
from jax.experimental.pallas import tpu as pltpu

F32 = jnp.float32
BF16 = jnp.bfloat16

D_MODEL = 1024
BATCH = 16
SEQ = 2048
N_TOK = BATCH * SEQ
N_HEADS = 16
HEAD_DIM = 64
D_ATT = N_HEADS * HEAD_DIM
SSM_GROUP = 16
N_GROUPS = 64
STATE = 64
D_FF = 2816
EPS = 1e-6

LANES = 128
ROW_TILE = 512
TILES_PER_SEQ = SEQ // ROW_TILE
ATT_TQ = 256
LOG2E = math.log2(math.e)
HEAD_PAIRS = N_HEADS // 2
SCAN_STEPS = 32
SCAN_ROWS = SCAN_STEPS * BATCH
GROUP_BLOCK = 16
N_GROUP_BLOCKS = N_GROUPS // GROUP_BLOCK
GB_CH = GROUP_BLOCK * SSM_GROUP
GB_ST = GROUP_BLOCK * STATE
FF_CHUNK = D_FF // 2
VMEM_LIMIT = 56 * 1024 * 1024


def _const_spec(shape):
    nd = len(shape)
    return pl.BlockSpec(shape, lambda *_: (0,) * nd, pipeline_mode=pl.Buffered(1))


def _rmsnorm(xf, gain):
    return xf * lax.rsqrt(jnp.mean(xf * xf, axis=-1, keepdims=True) + EPS) * gain


def _dot(a, b):
    return jnp.dot(a, b, preferred_element_type=F32)


def _sigmoid(x):
    return 1.0 / (1.0 + jnp.exp(-x))


def _params(*sem):
    return pltpu.CompilerParams(dimension_semantics=sem, vmem_limit_bytes=VMEM_LIMIT)


def _fox_in_kernel(x_ref, g_ref, w_ref, wf_ref, bf_ref, qg_ref, kg_ref, e_ref,
                   q_ref, k_ref, v_ref, sg_ref, c_ref, carry_ref):
    i = pl.program_id(0)
    h = _rmsnorm(x_ref[...], g_ref[...]).astype(BF16)

    for dst, col0, gain in ((q_ref, 0, qg_ref), (k_ref, D_ATT, kg_ref)):
        for cg in range(D_ATT // 256):
            lo = cg * 256
            blk = _dot(h, w_ref[:, col0 + lo:col0 + lo + 256])
            ssq = _dot((blk * blk).astype(BF16), e_ref[...])
            dst[:, lo:lo + 256] = (
                blk * lax.rsqrt(ssq * (1.0 / HEAD_DIM) + EPS) * gain[...]).astype(BF16)

    v_ref[...] = _dot(h, w_ref[:, 2 * D_ATT:3 * D_ATT]).astype(BF16)
    sg_ref[...] = _sigmoid(_dot(h, w_ref[:, 3 * D_ATT:4 * D_ATT])).astype(BF16)

    z = _dot(h, wf_ref[...]) + bf_ref[...]
    log_f = jnp.minimum(z, 0.0) - jnp.log(1.0 + jnp.exp(-jnp.abs(z)))
    rows = lax.broadcasted_iota(jnp.int32, (ROW_TILE, ROW_TILE), 0)
    cols = lax.broadcasted_iota(jnp.int32, (ROW_TILE, ROW_TILE), 1)
    tril = (rows >= cols).astype(BF16)
    hi = log_f.astype(BF16)
    r1 = log_f - hi.astype(F32)
    mid = r1.astype(BF16)
    lo3 = (r1 - mid.astype(F32)).astype(BF16)
    cum = _dot(tril, hi) + _dot(tril, mid) + _dot(tril, lo3)

    @pl.when(i % TILES_PER_SEQ == 0)
    def _():
        carry_ref[...] = jnp.zeros_like(carry_ref)

    cum = cum + carry_ref[...]
    c_ref[...] = cum * LOG2E
    carry_ref[...] = cum[ROW_TILE - 1:ROW_TILE, :]


def _fox_in(x, pre_gain, w_main, w_f, b_f, q_gain, k_gain, e_blk):
    row_spec = pl.BlockSpec((ROW_TILE, D_MODEL), lambda i: (i, 0))
    act = jax.ShapeDtypeStruct((N_TOK, D_ATT), BF16)
    return pl.pallas_call(
        _fox_in_kernel,
        grid=(N_TOK // ROW_TILE,),
        in_specs=[
            row_spec,
            _const_spec((1, D_MODEL)),
            _const_spec((D_MODEL, 4 * D_ATT)),
            _const_spec((D_MODEL, LANES)),
            _const_spec((1, LANES)),
            _const_spec((1, 256)),
            _const_spec((1, 256)),
            _const_spec((256, 256)),
        ],
        out_specs=[row_spec, row_spec, row_spec, row_spec,
                   pl.BlockSpec((ROW_TILE, LANES), lambda i: (i, 0))],
        out_shape=[act, act, act, act, jax.ShapeDtypeStruct((N_TOK, LANES), F32)],
        scratch_shapes=[pltpu.VMEM((1, LANES), F32)],
        compiler_params=_params("arbitrary"),
        name="fox_in",
    )(x, pre_gain, w_main, w_f, b_f, q_gain, k_gain, e_blk)


def _attn_kernel(q_ref, k_ref, v_ref, ck_ref, o_ref):
    lane = lax.broadcasted_iota(jnp.int32, (SEQ, LANES), 1)
    low_half = lane < HEAD_DIM
    q = q_ref[...].astype(F32)
    v = v_ref[...].astype(F32)
    k = k_ref[...]
    qm = (jnp.where(low_half, q, 0.0).astype(BF16), jnp.where(low_half, 0.0, q).astype(BF16))
    vm = (jnp.where(low_half, v, 1.0).astype(BF16), jnp.where(low_half, 1.0, v).astype(BF16))
    row = lax.broadcasted_iota(jnp.int32, (ATT_TQ, ATT_TQ), 0)
    col = lax.broadcasted_iota(jnp.int32, (ATT_TQ, ATT_TQ), 1)
    causal = row >= col
    low_q = lax.broadcasted_iota(jnp.int32, (ATT_TQ, LANES), 1) < HEAD_DIM
    nt = (((1,), (1,)), ((), ()))

    for qi in range(SEQ // ATT_TQ):
        k0 = qi * ATT_TQ
        outs = []
        for hh in range(2):
            qb = qm[hh][k0:k0 + ATT_TQ]
            s_d = lax.dot_general(qb, k[k0:k0 + ATT_TQ], nt, preferred_element_type=F32)
            s_d = jnp.where(causal, s_d - ck_ref[hh:hh + 1, k0:k0 + ATT_TQ], -1e30)
            m = jnp.max(s_d, axis=-1, keepdims=True)
            if qi > 0:
                s_o = lax.dot_general(qb, k[:k0], nt, preferred_element_type=F32)
                s_o = s_o - ck_ref[hh:hh + 1, :k0]
                m = jnp.maximum(m, jnp.max(s_o, axis=-1, keepdims=True))
            acc = _dot(jnp.exp2(s_d - m).astype(BF16), vm[hh][k0:k0 + ATT_TQ])
            if qi > 0:
                acc = acc + _dot(jnp.exp2(s_o - m).astype(BF16), vm[hh][:k0])
            outs.append(acc / pltpu.roll(acc, HEAD_DIM, 1))
        o_ref[k0:k0 + ATT_TQ, :] = jnp.where(low_q, outs[0], outs[1]).astype(BF16)


def _attention(q, k, v, ck):
    seq_spec = pl.BlockSpec((SEQ, LANES), lambda b, j: (b, j))
    return pl.pallas_call(
        _attn_kernel,
        grid=(BATCH, HEAD_PAIRS),
        in_specs=[seq_spec, seq_spec, seq_spec,
                  pl.BlockSpec((None, None, 2, SEQ), lambda b, j: (b, j, 0, 0))],
        out_specs=seq_spec,
        out_shape=jax.ShapeDtypeStruct((N_TOK, D_ATT), BF16),
        compiler_params=_params("arbitrary", "arbitrary"),
        name="fox_attention",
    )(q, k, v, ck)


def _ffn_body(x, pre_ref, wg_ref, wu_ref, wd_ref, post_ref):
    h = _rmsnorm(x, pre_ref[...]).astype(BF16)
    acc = None
    for c0 in range(0, D_FF, FF_CHUNK):
        g = _dot(h, wg_ref[:, c0:c0 + FF_CHUNK])
        u = _dot(h, wu_ref[:, c0:c0 + FF_CHUNK])
        a = (g * _sigmoid(g) * u).astype(BF16)
        part = _dot(a, wd_ref[c0:c0 + FF_CHUNK, :])
        acc = part if acc is None else acc + part
    return x + _rmsnorm(acc, post_ref[...])


def _row_spec():
    return pl.BlockSpec((ROW_TILE, D_MODEL), lambda i: (i, 0))


def _ffn_specs():
    return [_const_spec((1, D_MODEL)), _const_spec((D_MODEL, D_FF)),
            _const_spec((D_MODEL, D_FF)), _const_spec((D_FF, D_MODEL)),
            _const_spec((1, D_MODEL))]


def _layer0_tail_kernel(o_ref, sg_ref, x_ref, wo_ref, mpost_ref,
                        fpre_ref, wg_ref, wu_ref, wd_ref, fpost_ref,
                        spre_ref, wsin_ref, x2_ref, u_ref):
    m = _dot(o_ref[...] * sg_ref[...], wo_ref[...])
    x1 = x_ref[...] + _rmsnorm(m, mpost_ref[...])
    x2 = _ffn_body(x1, fpre_ref, wg_ref, wu_ref, wd_ref, fpost_ref)
    x2_ref[...] = x2
    u_ref[...] = _dot(_rmsnorm(x2, spre_ref[...]).astype(BF16), wsin_ref[...]).astype(BF16)


def _layer0_tail(o, sg, x, w_out, mix_post, ffn_w, s5_pre, s5_w_in):
    return pl.pallas_call(
        _layer0_tail_kernel,
        grid=(N_TOK // ROW_TILE,),
        in_specs=[_row_spec(), _row_spec(), _row_spec(),
                  _const_spec((D_ATT, D_MODEL)), _const_spec((1, D_MODEL)),
                  *_ffn_specs(),
                  _const_spec((1, D_MODEL)), _const_spec((D_MODEL, D_MODEL))],
        out_specs=[_row_spec(), _row_spec()],
        out_shape=[jax.ShapeDtypeStruct((N_TOK, D_MODEL), F32),
                   jax.ShapeDtypeStruct((N_TOK, D_MODEL), BF16)],
        compiler_params=_params("arbitrary"),
        name="layer0_tail",
    )(o, sg, x, w_out, mix_post, *ffn_w, s5_pre, s5_w_in)


def _layer1_tail_kernel(y_ref, x_ref, wglu_ref, wo_ref, mpost_ref,
                        fpre_ref, wg_ref, wu_ref, wd_ref, fpost_ref, out_ref):
    y = y_ref[...]
    gated = (y.astype(F32) * _sigmoid(_dot(y, wglu_ref[...]))).astype(BF16)
    x3 = x_ref[...] + _rmsnorm(_dot(gated, wo_ref[...]), mpost_ref[...])
    out_ref[...] = _ffn_body(x3, fpre_ref, wg_ref, wu_ref, wd_ref, fpost_ref)


def _layer1_tail(y, x, w_glu, w_out, mix_post, ffn_w):
    return pl.pallas_call(
        _layer1_tail_kernel,
        grid=(N_TOK // ROW_TILE,),
        in_specs=[_row_spec(), _row_spec(),
                  _const_spec((D_MODEL, D_MODEL)), _const_spec((D_MODEL, D_MODEL)),
                  _const_spec((1, D_MODEL)), *_ffn_specs()],
        out_specs=_row_spec(),
        out_shape=jax.ShapeDtypeStruct((N_TOK, D_MODEL), F32),
        compiler_params=_params("arbitrary"),
        name="layer1_tail",
    )(y, x, w_glu, w_out, mix_post, *ffn_w)


def _s5_disc_kernel(ldt_ref, lre_ref, lim_ref, bre_ref, bim_ref,
                    are_ref, aim_ref, bbre_ref, bbim_ref):
    dt = jnp.exp(ldt_ref[...])
    lam_re = lre_ref[...]
    lam_im = lim_ref[...]
    mag = jnp.exp(lam_re * dt)
    a_re = mag * jnp.cos(lam_im * dt)
    a_im = mag * jnp.sin(lam_im * dt)
    den = lam_re * lam_re + lam_im * lam_im
    n_re = a_re - 1.0
    z_re = (n_re * lam_re + a_im * lam_im) / den
    z_im = (a_im * lam_re - n_re * lam_im) / den
    are_ref[...] = a_re
    aim_ref[...] = a_im
    b_re = bre_ref[...]
    b_im = bim_ref[...]
    bbre_ref[...] = z_re * b_re - z_im * b_im
    bbim_ref[...] = z_re * b_im + z_im * b_re


def _s5_discretise(log_dt, lam_re, lam_im, b_re_t, b_im_t):
    gs = jax.ShapeDtypeStruct((N_GROUPS, 1, STATE), F32)
    gcs = jax.ShapeDtypeStruct((N_GROUPS, SSM_GROUP, STATE), F32)
    return pl.pallas_call(
        _s5_disc_kernel,
        out_shape=[gs, gs, gcs, gcs],
        name="s5_discretise",
    )(log_dt, lam_re, lam_im, b_re_t, b_im_t)


def _s5_scan_kernel(u_ref, p_ref, pt_ref, bre_ref, bim_ref, cre_ref, cim_ref,
                    are_ref, aim_ref, d_ref, y_ref, xr_ref, xi_ref, sr_ref, si_ref):
    @pl.when(pl.program_id(0) == 0)
    def _():
        sr_ref[...] = jnp.zeros_like(sr_ref)
        si_ref[...] = jnp.zeros_like(si_ref)

    u_f32 = _dot(p_ref[...], u_ref[...].reshape(SCAN_ROWS, D_MODEL))
    u = u_f32.astype(BF16)
    ys = []
    for g in range(N_GROUP_BLOCKS):
        ch = slice(g * GB_CH, (g + 1) * GB_CH)
        xr_ref[g] = _dot(u[:, ch], bre_ref[g])
        xi_ref[g] = _dot(u[:, ch], bim_ref[g])
        a_re = jnp.broadcast_to(are_ref[g], (BATCH, GB_ST))
        a_im = jnp.broadcast_to(aim_ref[g], (BATCH, GB_ST))
        s_re = sr_ref[g]
        s_im = si_ref[g]
        for t in range(SCAN_STEPS):
            rows = slice(t * BATCH, (t + 1) * BATCH)
            n_re = a_re * s_re - a_im * s_im + xr_ref[g, rows, :]
            n_im = a_re * s_im + a_im * s_re + xi_ref[g, rows, :]
            xr_ref[g, rows, :] = n_re
            xi_ref[g, rows, :] = n_im
            s_re, s_im = n_re, n_im
        sr_ref[g] = s_re
        si_ref[g] = s_im
        y = (_dot(xr_ref[g].astype(BF16), cre_ref[g])
             - _dot(xi_ref[g].astype(BF16), cim_ref[g]))
        y = y + d_ref[:, ch] * u_f32[:, ch]
        y = 0.5 * y * (1.0 + jnp.tanh(math.sqrt(2.0 / math.pi) * (y + 0.044715 * (y * y * y))))
        ys.append(y.astype(BF16))
    y_bm = _dot(pt_ref[...], jnp.concatenate(ys, axis=1)).astype(BF16)
    y_ref[...] = y_bm.reshape(BATCH, SCAN_STEPS, D_MODEL)


def _time_major_permutation():
    r = np.arange(SCAN_ROWS)
    src = (r % BATCH) * SCAN_STEPS + r // BATCH
    return (src[:, None] == np.arange(SCAN_ROWS)[None, :]).astype(np.float32)


def _s5_scan(u, b_re, b_im, c_re, c_im, a_re, a_im, d_skip):
    perm = _time_major_permutation()
    tile = pl.BlockSpec((BATCH, SCAN_STEPS, D_MODEL), lambda t: (0, t, 0))
    sq = (SCAN_ROWS, SCAN_ROWS)
    b_shape = (N_GROUP_BLOCKS, GB_CH, GB_ST)
    c_shape = (N_GROUP_BLOCKS, GB_ST, GB_CH)
    a_shape = (N_GROUP_BLOCKS, 1, GB_ST)
    return pl.pallas_call(
        _s5_scan_kernel,
        grid=(SEQ // SCAN_STEPS,),
        in_specs=[tile, _const_spec(sq), _const_spec(sq),
                  _const_spec(b_shape), _const_spec(b_shape),
                  _const_spec(c_shape), _const_spec(c_shape),
                  _const_spec(a_shape), _const_spec(a_shape),
                  _const_spec((1, D_MODEL))],
        out_specs=tile,
        out_shape=jax.ShapeDtypeStruct((BATCH, SEQ, D_MODEL), BF16),
        scratch_shapes=[pltpu.VMEM((N_GROUP_BLOCKS, SCAN_ROWS, GB_ST), F32),
                        pltpu.VMEM((N_GROUP_BLOCKS, SCAN_ROWS, GB_ST), F32),
                        pltpu.VMEM((N_GROUP_BLOCKS, BATCH, GB_ST), F32),
                        pltpu.VMEM((N_GROUP_BLOCKS, BATCH, GB_ST), F32)],
        compiler_params=_params("arbitrary"),
        name="s5_scan",
    )(u.reshape(BATCH, SEQ, D_MODEL), jnp.asarray(perm, BF16), jnp.asarray(perm.T, BF16),
      b_re, b_im, c_re, c_im, a_re, a_im, d_skip)


def _block_diag_in(bb_t):
    x = bb_t.reshape(N_GROUP_BLOCKS, GROUP_BLOCK, SSM_GROUP, STATE)
    eye = jnp.eye(GROUP_BLOCK, dtype=bb_t.dtype)
    x = jnp.einsum('klcp,lm->klcmp', x, eye)
    return x.reshape(N_GROUP_BLOCKS, GB_CH, GB_ST).astype(BF16)


def _block_diag_out(c):
    x = c.reshape(N_GROUP_BLOCKS, GROUP_BLOCK, SSM_GROUP, STATE)
    eye = jnp.eye(GROUP_BLOCK, dtype=c.dtype)
    x = jnp.einsum('klcp,lm->klpmc', x, eye)
    return x.reshape(N_GROUP_BLOCKS, GB_ST, GB_CH).astype(BF16)


def kernel(x, fox_w_in, fox_b_f, fox_q_gain, fox_k_gain, fox_w_out, s5_w_in, s5_log_dt, s5_lam_re, s5_lam_im, s5_b_re, s5_b_im, s5_c_re, s5_c_im, s5_d, s5_w_glu, s5_w_out, mix_pre_gain, mix_post_gain, ffn_pre_gain, ffn_post_gain, ffn_w_gate, ffn_w_up, ffn_w_down):
    row = lambda a: a.reshape(1, -1).astype(F32)
    x0 = x.reshape(N_TOK, D_MODEL)

    def ffn_weights(i):
        return (row(ffn_pre_gain[i]), ffn_w_gate[i].astype(BF16), ffn_w_up[i].astype(BF16),
                ffn_w_down[i].astype(BF16), row(ffn_post_gain[i]))

    w_in = fox_w_in[0]
    w_main = w_in[:, :4 * D_ATT].astype(BF16)
    w_f = jnp.pad(w_in[:, 4 * D_ATT:], ((0, 0), (0, LANES - N_HEADS))).astype(BF16)
    b_f = jnp.pad(fox_b_f[0], (0, LANES - N_HEADS)).reshape(1, LANES).astype(F32)
    reps = 256 // HEAD_DIM
    q_gain = row(jnp.tile(fox_q_gain[0] * (HEAD_DIM ** -0.5 * LOG2E), reps))
    k_gain = row(jnp.tile(fox_k_gain[0], reps))
    head_of = jnp.arange(256) // HEAD_DIM
    e_blk = (head_of[:, None] == head_of[None, :]).astype(BF16)

    q, k, v, sg, c = _fox_in(x0, row(mix_pre_gain[0]), w_main, w_f, b_f, q_gain, k_gain, e_blk)
    ck = c[:, :N_HEADS].reshape(BATCH, SEQ, HEAD_PAIRS, 2).transpose(0, 2, 3, 1)
    o = _attention(q, k, v, ck)
    x2, u = _layer0_tail(o, sg, x0, fox_w_out[0].astype(BF16), row(mix_post_gain[0]),
                         ffn_weights(0), row(mix_pre_gain[1]), s5_w_in[0].astype(BF16))

    gps = (N_GROUPS, 1, STATE)
    a_re, a_im, bb_re, bb_im = _s5_discretise(
        s5_log_dt[0].reshape(N_GROUPS, 1, 1).astype(F32),
        s5_lam_re[0].reshape(gps).astype(F32), s5_lam_im[0].reshape(gps).astype(F32),
        s5_b_re[0].transpose(0, 2, 1).astype(F32), s5_b_im[0].transpose(0, 2, 1).astype(F32))
    y = _s5_scan(u, _block_diag_in(bb_re), _block_diag_in(bb_im),
                 _block_diag_out(s5_c_re[0]), _block_diag_out(s5_c_im[0]),
                 a_re.reshape(N_GROUP_BLOCKS, 1, GB_ST), a_im.reshape(N_GROUP_BLOCKS, 1, GB_ST),
                 row(s5_d[0]))
    x4 = _layer1_tail(y.reshape(N_TOK, D_MODEL), x2, s5_w_glu[0].astype(BF16),
                      s5_w_out[0].astype(BF16), row(mix_post_gain[1]), ffn_weights(1))
    return x4.reshape(BATCH, SEQ, D_MODEL)
```

```python
import math

import numpy as np

import jax
import jax.numpy as jnp
from jax import lax
from jax.experimental import pallas as pl
from jax.experimental.pallas import tpu as pltpu

F32 = jnp.float32
BF16 = jnp.bfloat16

D_MODEL = 1024
BATCH = 16
SEQ = 2048
N_TOK = BATCH * SEQ
N_HEADS = 16
HEAD_DIM = 64
D_ATT = N_HEADS * HEAD_DIM
SSM_GROUP = 16
N_GROUPS = 64
STATE = 64
D_FF = 2816
EPS = 1e-6

LANES = 128
ROW_TILE = 512
TILES_PER_SEQ = SEQ // ROW_TILE
ATT_TQ = 256
ATT_LOOKAHEAD = 2
LOG2E = math.log2(math.e)
HEAD_PAIRS = N_HEADS // 2
SCAN_STEPS = 32
SCAN_ROWS = SCAN_STEPS * BATCH
GROUP_BLOCK = 16
N_GROUP_BLOCKS = N_GROUPS // GROUP_BLOCK
GB_CH = GROUP_BLOCK * SSM_GROUP
GB_ST = GROUP_BLOCK * STATE
MXU_DIM = 256
_FF_SPLIT = (D_FF // MXU_DIM // 2) * MXU_DIM
FF_CHUNKS = ((0, _FF_SPLIT), (_FF_SPLIT, D_FF))
VMEM_LIMIT = 56 * 1024 * 1024


def _const_spec(shape):
    nd = len(shape)
    return pl.BlockSpec(shape, lambda *_: (0,) * nd, pipeline_mode=pl.Buffered(1))


def _rmsnorm(xf, gain):
    return xf * lax.rsqrt(jnp.mean(xf * xf, axis=-1, keepdims=True) + EPS) * gain


def _dot(a, b):
    return jnp.dot(a, b, preferred_element_type=F32)


def _sigmoid(x):
    return 1.0 / (1.0 + jnp.exp(-x))


def _params(*sem):
    return pltpu.CompilerParams(dimension_semantics=sem, vmem_limit_bytes=VMEM_LIMIT)


def _fox_in_kernel(x_ref, g_ref, w_ref, wf_ref, bf_ref, qg_ref, kg_ref, e_ref,
                   q_ref, k_ref, v_ref, sg_ref, c_ref, carry_ref):
    i = pl.program_id(0)
    h = _rmsnorm(x_ref[...], g_ref[...]).astype(BF16)

    q_raw = _dot(h, w_ref[:, :D_ATT])
    k_raw = _dot(h, w_ref[:, D_ATT:2 * D_ATT])
    v_ref[...] = _dot(h, w_ref[:, 2 * D_ATT:3 * D_ATT]).astype(BF16)
    sg_ref[...] = _sigmoid(_dot(h, w_ref[:, 3 * D_ATT:4 * D_ATT])).astype(BF16)

    for dst, raw, gain in ((q_ref, q_raw, qg_ref), (k_ref, k_raw, kg_ref)):
        for lo in range(0, D_ATT, 256):
            blk = raw[:, lo:lo + 256]
            ssq = _dot((blk * blk).astype(BF16), e_ref[...])
            dst[:, lo:lo + 256] = (
                blk * lax.rsqrt(ssq * (1.0 / HEAD_DIM) + EPS) * gain[...]).astype(BF16)

    z = _dot(h, wf_ref[...]) + bf_ref[...]
    log_f = jnp.minimum(z, 0.0) - jnp.log(1.0 + jnp.exp(-jnp.abs(z)))
    rows = lax.broadcasted_iota(jnp.int32, (ROW_TILE, ROW_TILE), 0)
    cols = lax.broadcasted_iota(jnp.int32, (ROW_TILE, ROW_TILE), 1)
    tril = (rows >= cols).astype(BF16)
    hi = log_f.astype(BF16)
    r1 = log_f - hi.astype(F32)
    mid = r1.astype(BF16)
    lo3 = (r1 - mid.astype(F32)).astype(BF16)
    cum = _dot(tril, hi) + _dot(tril, mid) + _dot(tril, lo3)

    @pl.when(i % TILES_PER_SEQ == 0)
    def _():
        carry_ref[...] = jnp.zeros_like(carry_ref)

    cum = cum + carry_ref[...]
    c_ref[...] = cum * LOG2E
    carry_ref[...] = cum[ROW_TILE - 1:ROW_TILE, :]


def _fox_in(x, pre_gain, w_main, w_f, b_f, q_gain, k_gain, e_blk):
    row_spec = pl.BlockSpec((ROW_TILE, D_MODEL), lambda i: (i, 0))
    act = jax.ShapeDtypeStruct((N_TOK, D_ATT), BF16)
    return pl.pallas_call(
        _fox_in_kernel,
        grid=(N_TOK // ROW_TILE,),
        in_specs=[
            row_spec,
            _const_spec((1, D_MODEL)),
            _const_spec((D_MODEL, 4 * D_ATT)),
            _const_spec((D_MODEL, LANES)),
            _const_spec((1, LANES)),
            _const_spec((1, 256)),
            _const_spec((1, 256)),
            _const_spec((256, 256)),
        ],
        out_specs=[row_spec, row_spec, row_spec, row_spec,
                   pl.BlockSpec((ROW_TILE, LANES), lambda i: (i, 0))],
        out_shape=[act, act, act, act, jax.ShapeDtypeStruct((N_TOK, LANES), F32)],
        scratch_shapes=[pltpu.VMEM((1, LANES), F32)],
        compiler_params=_params("arbitrary"),
        name="fox_in",
    )(x, pre_gain, w_main, w_f, b_f, q_gain, k_gain, e_blk)


def _attn_kernel(q_ref, k_ref, v_ref, ck_ref, o_ref):
    lane = lax.broadcasted_iota(jnp.int32, (SEQ, LANES), 1)
    low_half = lane < HEAD_DIM
    q = q_ref[...].astype(F32)
    v = v_ref[...].astype(F32)
    k = k_ref[...]
    qm = (jnp.where(low_half, q, 0.0).astype(BF16), jnp.where(low_half, 0.0, q).astype(BF16))
    vm = (jnp.where(low_half, v, 1.0).astype(BF16), jnp.where(low_half, 1.0, v).astype(BF16))
    row = lax.broadcasted_iota(jnp.int32, (ATT_TQ, ATT_TQ), 0)
    col = lax.broadcasted_iota(jnp.int32, (ATT_TQ, ATT_TQ), 1)
    causal = row >= col
    low_q = lax.broadcasted_iota(jnp.int32, (ATT_TQ, LANES), 1) < HEAD_DIM
    nt = (((1,), (1,)), ((), ()))

    def scores(qi, hh):
        k0 = qi * ATT_TQ
        qb = qm[hh][k0:k0 + ATT_TQ]
        s_d = lax.dot_general(qb, k[k0:k0 + ATT_TQ], nt, preferred_element_type=F32)
        s_o = lax.dot_general(qb, k[:k0], nt, preferred_element_type=F32) if qi else None
        return s_d, s_o

    def attend(qi, hh, s_d, s_o):
        k0 = qi * ATT_TQ
        s_d = jnp.where(causal, s_d - ck_ref[hh:hh + 1, k0:k0 + ATT_TQ], -1e30)
        m = jnp.max(s_d, axis=-1, keepdims=True)
        if qi:
            s_o = s_o - ck_ref[hh:hh + 1, :k0]
            m = jnp.maximum(m, jnp.max(s_o, axis=-1, keepdims=True))
        acc = _dot(jnp.exp2(s_d - m).astype(BF16), vm[hh][k0:k0 + ATT_TQ])
        if qi:
            acc = acc + _dot(jnp.exp2(s_o - m).astype(BF16), vm[hh][:k0])
        return acc / pltpu.roll(acc, HEAD_DIM, 1)

    items = [(qi, hh) for qi in range(SEQ // ATT_TQ) for hh in range(2)]
    ahead = [scores(*it) for it in items[:ATT_LOOKAHEAD]]
    outs = []
    for n, (qi, hh) in enumerate(items):
        current = ahead.pop(0)
        if n + ATT_LOOKAHEAD < len(items):
            ahead.append(scores(*items[n + ATT_LOOKAHEAD]))
        outs.append(attend(qi, hh, *current))
        if hh == 1:
            k0 = qi * ATT_TQ
            o_ref[k0:k0 + ATT_TQ, :] = jnp.where(low_q, outs[0], outs[1]).astype(BF16)
            outs = []


def _attention(q, k, v, ck):
    seq_spec = pl.BlockSpec((SEQ, LANES), lambda b, j: (b, j))
    return pl.pallas_call(
        _attn_kernel,
        grid=(BATCH, HEAD_PAIRS),
        in_specs=[seq_spec, seq_spec, seq_spec,
                  pl.BlockSpec((None, None, 2, SEQ), lambda b, j: (b, j, 0, 0))],
        out_specs=seq_spec,
        out_shape=jax.ShapeDtypeStruct((N_TOK, D_ATT), BF16),
        compiler_params=_params("arbitrary", "arbitrary"),
        name="fox_attention",
    )(q, k, v, ck)


def _ffn_body(x, pre_ref, wg_ref, wu_ref, wd_ref, post_ref):
    h = _rmsnorm(x, pre_ref[...]).astype(BF16)
    acc = None
    for c0, c1 in FF_CHUNKS:
        g = _dot(h, wg_ref[:, c0:c1])
        u = _dot(h, wu_ref[:, c0:c1])
        a = (g * _sigmoid(g) * u).astype(BF16)
        part = _dot(a, wd_ref[c0:c1, :])
        acc = part if acc is None else acc + part
    return x + _rmsnorm(acc, post_ref[...])


def _row_spec():
    return pl.BlockSpec((ROW_TILE, D_MODEL), lambda i: (i, 0))


def _ffn_specs():
    return [_const_spec((1, D_MODEL)), _const_spec((D_MODEL, D_FF)),
            _const_spec((D_MODEL, D_FF)), _const_spec((D_FF, D_MODEL)),
            _const_spec((1, D_MODEL))]


def _layer0_tail_kernel(o_ref, sg_ref, x_ref, wo_ref, mpost_ref,
                        fpre_ref, wg_ref, wu_ref, wd_ref, fpost_ref,
                        spre_ref, wsin_ref, x2_ref, u_ref):
    m = _dot(o_ref[...] * sg_ref[...], wo_ref[...])
    x1 = x_ref[...] + _rmsnorm(m, mpost_ref[...])
    x2 = _ffn_body(x1, fpre_ref, wg_ref, wu_ref, wd_ref, fpost_ref)
    x2_ref[...] = x2
    u_ref[...] = _dot(_rmsnorm(x2, spre_ref[...]).astype(BF16), wsin_ref[...]).astype(BF16)


def _layer0_tail(o, sg, x, w_out, mix_post, ffn_w, s5_pre, s5_w_in):
    return pl.pallas_call(
        _layer0_tail_kernel,
        grid=(N_TOK // ROW_TILE,),
        in_specs=[_row_spec(), _row_spec(), _row_spec(),
                  _const_spec((D_ATT, D_MODEL)), _const_spec((1, D_MODEL)),
                  *_ffn_specs(),
                  _const_spec((1, D_MODEL)), _const_spec((D_MODEL, D_MODEL))],
        out_specs=[_row_spec(), _row_spec()],
        out_shape=[jax.ShapeDtypeStruct((N_TOK, D_MODEL), F32),
                   jax.ShapeDtypeStruct((N_TOK, D_MODEL), BF16)],
        compiler_params=_params("arbitrary"),
        name="layer0_tail",
    )(o, sg, x, w_out, mix_post, *ffn_w, s5_pre, s5_w_in)


def _layer1_tail_kernel(y_ref, x_ref, wglu_ref, wo_ref, mpost_ref,
                        fpre_ref, wg_ref, wu_ref, wd_ref, fpost_ref, out_ref):
    y = y_ref[...]
    gated = (y.astype(F32) * _sigmoid(_dot(y, wglu_ref[...]))).astype(BF16)
    x3 = x_ref[...] + _rmsnorm(_dot(gated, wo_ref[...]), mpost_ref[...])
    out_ref[...] = _ffn_body(x3, fpre_ref, wg_ref, wu_ref, wd_ref, fpost_ref)


def _layer1_tail(y, x, w_glu, w_out, mix_post, ffn_w):
    return pl.pallas_call(
        _layer1_tail_kernel,
        grid=(N_TOK // ROW_TILE,),
        in_specs=[_row_spec(), _row_spec(),
                  _const_spec((D_MODEL, D_MODEL)), _const_spec((D_MODEL, D_MODEL)),
                  _const_spec((1, D_MODEL)), *_ffn_specs()],
        out_specs=_row_spec(),
        out_shape=jax.ShapeDtypeStruct((N_TOK, D_MODEL), F32),
        compiler_params=_params("arbitrary"),
        name="layer1_tail",
    )(y, x, w_glu, w_out, mix_post, *ffn_w)


def _s5_disc_kernel(ldt_ref, lre_ref, lim_ref, bre_ref, bim_ref,
                    are_ref, aim_ref, bbre_ref, bbim_ref):
    dt = jnp.exp(ldt_ref[...])
    lam_re = lre_ref[...]
    lam_im = lim_ref[...]
    mag = jnp.exp(lam_re * dt)
    a_re = mag * jnp.cos(lam_im * dt)
    a_im = mag * jnp.sin(lam_im * dt)
    den = lam_re * lam_re + lam_im * lam_im
    n_re = a_re - 1.0
    z_re = (n_re * lam_re + a_im * lam_im) / den
    z_im = (a_im * lam_re - n_re * lam_im) / den
    are_ref[...] = a_re
    aim_ref[...] = a_im
    b_re = bre_ref[...]
    b_im = bim_ref[...]
    bbre_ref[...] = z_re * b_re - z_im * b_im
    bbim_ref[...] = z_re * b_im + z_im * b_re


def _s5_discretise(log_dt, lam_re, lam_im, b_re_t, b_im_t):
    gs = jax.ShapeDtypeStruct((N_GROUPS, 1, STATE), F32)
    gcs = jax.ShapeDtypeStruct((N_GROUPS, SSM_GROUP, STATE), F32)
    return pl.pallas_call(
        _s5_disc_kernel,
        out_shape=[gs, gs, gcs, gcs],
        name="s5_discretise",
    )(log_dt, lam_re, lam_im, b_re_t, b_im_t)


def _s5_scan_kernel(u_ref, p_ref, pt_ref, bre_ref, bim_ref, cre_ref, cim_ref,
                    are_ref, aim_ref, d_ref, y_ref, xr_ref, xi_ref, sr_ref, si_ref):
    @pl.when(pl.program_id(0) == 0)
    def _():
        sr_ref[...] = jnp.zeros_like(sr_ref)
        si_ref[...] = jnp.zeros_like(si_ref)

    u_f32 = _dot(p_ref[...], u_ref[...].reshape(SCAN_ROWS, D_MODEL))
    u = u_f32.astype(BF16)
    ys = []
    for g in range(N_GROUP_BLOCKS):
        ch = slice(g * GB_CH, (g + 1) * GB_CH)
        xr_ref[g] = _dot(u[:, ch], bre_ref[g])
        xi_ref[g] = _dot(u[:, ch], bim_ref[g])
        a_re = jnp.broadcast_to(are_ref[g], (BATCH, GB_ST))
        a_im = jnp.broadcast_to(aim_ref[g], (BATCH, GB_ST))
        s_re = sr_ref[g]
        s_im = si_ref[g]
        for t in range(SCAN_STEPS):
            rows = slice(t * BATCH, (t + 1) * BATCH)
            n_re = a_re * s_re - a_im * s_im + xr_ref[g, rows, :]
            n_im = a_re * s_im + a_im * s_re + xi_ref[g, rows, :]
            xr_ref[g, rows, :] = n_re
            xi_ref[g, rows, :] = n_im
            s_re, s_im = n_re, n_im
        sr_ref[g] = s_re
        si_ref[g] = s_im
        y = (_dot(xr_ref[g].astype(BF16), cre_ref[g])
             - _dot(xi_ref[g].astype(BF16), cim_ref[g]))
        y = y + d_ref[:, ch] * u_f32[:, ch]
        y = 0.5 * y * (1.0 + jnp.tanh(math.sqrt(2.0 / math.pi) * (y + 0.044715 * (y * y * y))))
        ys.append(y.astype(BF16))
    y_bm = _dot(pt_ref[...], jnp.concatenate(ys, axis=1)).astype(BF16)
    y_ref[...] = y_bm.reshape(BATCH, SCAN_STEPS, D_MODEL)


def _time_major_permutation():
    r = np.arange(SCAN_ROWS)
    src = (r % BATCH) * SCAN_STEPS + r // BATCH
    return (src[:, None] == np.arange(SCAN_ROWS)[None, :]).astype(np.float32)


def _s5_scan(u, b_re, b_im, c_re, c_im, a_re, a_im, d_skip):
    perm = _time_major_permutation()
    tile = pl.BlockSpec((BATCH, SCAN_STEPS, D_MODEL), lambda t: (0, t, 0))
    sq = (SCAN_ROWS, SCAN_ROWS)
    b_shape = (N_GROUP_BLOCKS, GB_CH, GB_ST)
    c_shape = (N_GROUP_BLOCKS, GB_ST, GB_CH)
    a_shape = (N_GROUP_BLOCKS, 1, GB_ST)
    return pl.pallas_call(
        _s5_scan_kernel,
        grid=(SEQ // SCAN_STEPS,),
        in_specs=[tile, _const_spec(sq), _const_spec(sq),
                  _const_spec(b_shape), _const_spec(b_shape),
                  _const_spec(c_shape), _const_spec(c_shape),
                  _const_spec(a_shape), _const_spec(a_shape),
                  _const_spec((1, D_MODEL))],
        out_specs=tile,
        out_shape=jax.ShapeDtypeStruct((BATCH, SEQ, D_MODEL), BF16),
        scratch_shapes=[pltpu.VMEM((N_GROUP_BLOCKS, SCAN_ROWS, GB_ST), F32),
                        pltpu.VMEM((N_GROUP_BLOCKS, SCAN_ROWS, GB_ST), F32),
                        pltpu.VMEM((N_GROUP_BLOCKS, BATCH, GB_ST), F32),
                        pltpu.VMEM((N_GROUP_BLOCKS, BATCH, GB_ST), F32)],
        compiler_params=_params("arbitrary"),
        name="s5_scan",
    )(u.reshape(BATCH, SEQ, D_MODEL), jnp.asarray(perm, BF16), jnp.asarray(perm.T, BF16),
      b_re, b_im, c_re, c_im, a_re, a_im, d_skip)


def _block_diag_in(bb_t):
    x = bb_t.reshape(N_GROUP_BLOCKS, GROUP_BLOCK, SSM_GROUP, STATE)
    eye = jnp.eye(GROUP_BLOCK, dtype=bb_t.dtype)
    x = jnp.einsum('klcp,lm->klcmp', x, eye)
    return x.reshape(N_GROUP_BLOCKS, GB_CH, GB_ST).astype(BF16)


def _block_diag_out(c):
    x = c.reshape(N_GROUP_BLOCKS, GROUP_BLOCK, SSM_GROUP, STATE)
    eye = jnp.eye(GROUP_BLOCK, dtype=c.dtype)
    x = jnp.einsum('klcp,lm->klpmc', x, eye)
    return x.reshape(N_GROUP_BLOCKS, GB_ST, GB_CH).astype(BF16)


def kernel(x, fox_w_in, fox_b_f, fox_q_gain, fox_k_gain, fox_w_out, s5_w_in, s5_log_dt, s5_lam_re, s5_lam_im, s5_b_re, s5_b_im, s5_c_re, s5_c_im, s5_d, s5_w_glu, s5_w_out, mix_pre_gain, mix_post_gain, ffn_pre_gain, ffn_post_gain, ffn_w_gate, ffn_w_up, ffn_w_down):
    row = lambda a: a.reshape(1, -1).astype(F32)
    x0 = x.reshape(N_TOK, D_MODEL)

    def ffn_weights(i):
        return (row(ffn_pre_gain[i]), ffn_w_gate[i].astype(BF16), ffn_w_up[i].astype(BF16),
                ffn_w_down[i].astype(BF16), row(ffn_post_gain[i]))

    w_in = fox_w_in[0]
    w_main = w_in[:, :4 * D_ATT].astype(BF16)
    w_f = jnp.pad(w_in[:, 4 * D_ATT:], ((0, 0), (0, LANES - N_HEADS))).astype(BF16)
    b_f = jnp.pad(fox_b_f[0], (0, LANES - N_HEADS)).reshape(1, LANES).astype(F32)
    reps = 256 // HEAD_DIM
    q_gain = row(jnp.tile(fox_q_gain[0] * (HEAD_DIM ** -0.5 * LOG2E), reps))
    k_gain = row(jnp.tile(fox_k_gain[0], reps))
    head_of = jnp.arange(256) // HEAD_DIM
    e_blk = (head_of[:, None] == head_of[None, :]).astype(BF16)

    q, k, v, sg, c = _fox_in(x0, row(mix_pre_gain[0]), w_main, w_f, b_f, q_gain, k_gain, e_blk)
    ck = c[:, :N_HEADS].reshape(BATCH, SEQ, HEAD_PAIRS, 2).transpose(0, 2, 3, 1)
    o = _attention(q, k, v, ck)
    x2, u = _layer0_tail(o, sg, x0, fox_w_out[0].astype(BF16), row(mix_post_gain[0]),
                         ffn_weights(0), row(mix_pre_gain[1]), s5_w_in[0].astype(BF16))

    gps = (N_GROUPS, 1, STATE)
    a_re, a_im, bb_re, bb_im = _s5_discretise(
        s5_log_dt[0].reshape(N_GROUPS, 1, 1).astype(F32),
        s5_lam_re[0].reshape(gps).astype(F32), s5_lam_im[0].reshape(gps).astype(F32),
        s5_b_re[0].transpose(0, 2, 1).astype(F32), s5_b_im[0].transpose(0, 2, 1).astype(F32))
    y = _s5_scan(u, _block_diag_in(bb_re), _block_diag_in(bb_im),
                 _block_diag_out(s5_c_re[0]), _block_diag_out(s5_c_im[0]),
                 a_re.reshape(N_GROUP_BLOCKS, 1, GB_ST), a_im.reshape(N_GROUP_BLOCKS, 1, GB_ST),
                 row(s5_d[0]))
    x4 = _layer1_tail(y.reshape(N_TOK, D_MODEL), x2, s5_w_glu[0].astype(BF16),
                      s5_w_out[0].astype(BF16), row(mix_post_gain[1]), ffn_weights(1))
    return x4.reshape(BATCH, SEQ, D_MODEL)
```

```python
import math

import numpy as np

import jax
import jax.numpy as jnp
from jax import lax
from jax.experimental import pallas as pl
from jax.experimental.pallas import tpu as pltpu

F32 = jnp.float32
BF16 = jnp.bfloat16

D_MODEL = 1024
BATCH = 16
SEQ = 2048
N_TOK = BATCH * SEQ
N_HEADS = 16
HEAD_DIM = 64
D_ATT = N_HEADS * HEAD_DIM
SSM_GROUP = 16
N_GROUPS = 64
STATE = 64
D_FF = 2816
EPS = 1e-6

LANES = 128
ROW_TILE = 512
TILES_PER_SEQ = SEQ // ROW_TILE
ATT_TQ = 256
ATT_LOOKAHEAD = 2
ATT_ONES_ROWS = 16
LOG2E = math.log2(math.e)
HEAD_PAIRS = N_HEADS // 2
SCAN_STEPS = 32
SCAN_ROWS = SCAN_STEPS * BATCH
GROUP_BLOCK = 16
N_GROUP_BLOCKS = N_GROUPS // GROUP_BLOCK
GB_CH = GROUP_BLOCK * SSM_GROUP
GB_ST = GROUP_BLOCK * STATE
MXU_DIM = 256
_FF_SPLIT = (D_FF // MXU_DIM // 2) * MXU_DIM
FF_CHUNKS = ((0, _FF_SPLIT), (_FF_SPLIT, D_FF))
VMEM_LIMIT = 56 * 1024 * 1024


def _const_spec(shape):
    nd = len(shape)
    return pl.BlockSpec(shape, lambda *_: (0,) * nd, pipeline_mode=pl.Buffered(1))


def _rmsnorm(xf, gain):
    return xf * lax.rsqrt(jnp.mean(xf * xf, axis=-1, keepdims=True) + EPS) * gain


def _dot(a, b):
    return jnp.dot(a, b, preferred_element_type=F32)


def _sigmoid(x):
    return 1.0 / (1.0 + jnp.exp(-x))


def _params(*sem):
    return pltpu.CompilerParams(dimension_semantics=sem, vmem_limit_bytes=VMEM_LIMIT)


def _fox_in_kernel(x_ref, g_ref, w_ref, wf_ref, bf_ref, qg_ref, kg_ref, e_ref,
                   qt_ref, k_ref, vt_ref, sg_ref, c_ref, carry_ref):
    i = pl.program_id(0)
    h = _rmsnorm(x_ref[...], g_ref[...]).astype(BF16)

    q_raw = _dot(h, w_ref[:, :D_ATT])
    k_raw = _dot(h, w_ref[:, D_ATT:2 * D_ATT])
    vt_ref[...] = _dot(h, w_ref[:, 2 * D_ATT:3 * D_ATT]).T.astype(BF16)
    sg_ref[...] = _sigmoid(_dot(h, w_ref[:, 3 * D_ATT:4 * D_ATT])).astype(BF16)

    for raw, gain, transposed in ((q_raw, qg_ref, True), (k_raw, kg_ref, False)):
        for lo in range(0, D_ATT, 256):
            blk = raw[:, lo:lo + 256]
            ssq = _dot((blk * blk).astype(BF16), e_ref[...])
            normed = blk * lax.rsqrt(ssq * (1.0 / HEAD_DIM) + EPS) * gain[...]
            if transposed:
                qt_ref[lo:lo + 256, :] = normed.T.astype(BF16)
            else:
                k_ref[:, lo:lo + 256] = normed.astype(BF16)

    z = _dot(h, wf_ref[...]) + bf_ref[...]
    log_f = jnp.minimum(z, 0.0) - jnp.log(1.0 + jnp.exp(-jnp.abs(z)))
    rows = lax.broadcasted_iota(jnp.int32, (ROW_TILE, ROW_TILE), 0)
    cols = lax.broadcasted_iota(jnp.int32, (ROW_TILE, ROW_TILE), 1)
    tril = (rows >= cols).astype(BF16)
    hi = log_f.astype(BF16)
    r1 = log_f - hi.astype(F32)
    mid = r1.astype(BF16)
    lo3 = (r1 - mid.astype(F32)).astype(BF16)
    cum = _dot(tril, hi) + _dot(tril, mid) + _dot(tril, lo3)

    @pl.when(i % TILES_PER_SEQ == 0)
    def _():
        carry_ref[...] = jnp.zeros_like(carry_ref)

    cum = cum + carry_ref[...]
    c_ref[...] = cum * LOG2E
    carry_ref[...] = cum[ROW_TILE - 1:ROW_TILE, :]


def _fox_in(x, pre_gain, w_main, w_f, b_f, q_gain, k_gain, e_blk):
    row_spec = pl.BlockSpec((ROW_TILE, D_MODEL), lambda i: (i, 0))
    col_spec = pl.BlockSpec((D_ATT, ROW_TILE), lambda i: (0, i))
    act = jax.ShapeDtypeStruct((N_TOK, D_ATT), BF16)
    act_t = jax.ShapeDtypeStruct((D_ATT, N_TOK), BF16)
    return pl.pallas_call(
        _fox_in_kernel,
        grid=(N_TOK // ROW_TILE,),
        in_specs=[
            row_spec,
            _const_spec((1, D_MODEL)),
            _const_spec((D_MODEL, 4 * D_ATT)),
            _const_spec((D_MODEL, LANES)),
            _const_spec((1, LANES)),
            _const_spec((1, 256)),
            _const_spec((1, 256)),
            _const_spec((256, 256)),
        ],
        out_specs=[col_spec, row_spec, col_spec, row_spec,
                   pl.BlockSpec((ROW_TILE, LANES), lambda i: (i, 0))],
        out_shape=[act_t, act, act_t, act, jax.ShapeDtypeStruct((N_TOK, LANES), F32)],
        scratch_shapes=[pltpu.VMEM((1, LANES), F32)],
        compiler_params=_params("arbitrary"),
        name="fox_in",
    )(x, pre_gain, w_main, w_f, b_f, q_gain, k_gain, e_blk)


def _attn_kernel(qt_ref, k_ref, vt_ref, c_ref, o_ref):
    pair = pl.program_id(1)
    k = k_ref[...]
    qt = qt_ref[...].astype(F32)
    low_rows = lax.broadcasted_iota(jnp.int32, (LANES, SEQ), 0) < HEAD_DIM
    qmt = (jnp.where(low_rows, qt, 0.0).astype(BF16), jnp.where(low_rows, 0.0, qt).astype(BF16))
    ones = jnp.ones((ATT_ONES_ROWS, SEQ), BF16)
    vmt = tuple(jnp.concatenate([vt_ref[hh * HEAD_DIM:(hh + 1) * HEAD_DIM, :], ones], axis=0)
                for hh in range(2))
    c_all = c_ref[...]
    lane = lax.broadcasted_iota(jnp.int32, (SEQ, LANES), 1)
    cb = tuple(
        jnp.broadcast_to(
            jnp.sum(jnp.where(lane == 2 * pair + hh, c_all, 0.0), axis=-1, keepdims=True),
            (SEQ, ATT_TQ))
        for hh in range(2))
    key = lax.broadcasted_iota(jnp.int32, (ATT_TQ, ATT_TQ), 0)
    qry = lax.broadcasted_iota(jnp.int32, (ATT_TQ, ATT_TQ), 1)
    causal = key <= qry

    def scores(qi, hh):
        k0 = qi * ATT_TQ
        qcols = qmt[hh][:, k0:k0 + ATT_TQ]
        s_d = _dot(k[k0:k0 + ATT_TQ], qcols)
        s_o = _dot(k[:k0], qcols) if qi else None
        return s_d, s_o

    def attend(qi, hh, s_d, s_o):
        k0 = qi * ATT_TQ
        s_d = jnp.where(causal, s_d - cb[hh][k0:k0 + ATT_TQ], -1e30)
        m = jnp.max(s_d, axis=0, keepdims=True)
        if qi:
            s_o = s_o - cb[hh][:k0]
            m = jnp.maximum(m, jnp.max(s_o, axis=0, keepdims=True))
        acc = _dot(vmt[hh][:, k0:k0 + ATT_TQ], jnp.exp2(s_d - m).astype(BF16))
        if qi:
            acc = acc + _dot(vmt[hh][:, :k0], jnp.exp2(s_o - m).astype(BF16))
        return acc[:HEAD_DIM] / acc[HEAD_DIM:HEAD_DIM + 1]

    items = [(qi, hh) for qi in range(SEQ // ATT_TQ) for hh in range(2)]
    ahead = [scores(*it) for it in items[:ATT_LOOKAHEAD]]
    outs = []
    for n, (qi, hh) in enumerate(items):
        current = ahead.pop(0)
        if n + ATT_LOOKAHEAD < len(items):
            ahead.append(scores(*items[n + ATT_LOOKAHEAD]))
        outs.append(attend(qi, hh, *current))
        if hh == 1:
            k0 = qi * ATT_TQ
            o_ref[k0:k0 + ATT_TQ, :] = jnp.concatenate(outs, axis=0).T.astype(BF16)
            outs = []


def _attention(qt, k, vt, c):
    seq_spec = pl.BlockSpec((SEQ, LANES), lambda b, j: (b, j))
    seq_t_spec = pl.BlockSpec((LANES, SEQ), lambda b, j: (j, b))
    return pl.pallas_call(
        _attn_kernel,
        grid=(BATCH, HEAD_PAIRS),
        in_specs=[seq_t_spec, seq_spec, seq_t_spec,
                  pl.BlockSpec((SEQ, LANES), lambda b, j: (b, 0))],
        out_specs=seq_spec,
        out_shape=jax.ShapeDtypeStruct((N_TOK, D_ATT), BF16),
        compiler_params=_params("arbitrary", "arbitrary"),
        name="fox_attention",
    )(qt, k, vt, c)


def _ffn_body(x, pre_ref, wg_ref, wu_ref, wd_ref, post_ref):
    h = _rmsnorm(x, pre_ref[...]).astype(BF16)
    acc = None
    for c0, c1 in FF_CHUNKS:
        g = _dot(h, wg_ref[:, c0:c1])
        u = _dot(h, wu_ref[:, c0:c1])
        a = (g * _sigmoid(g) * u).astype(BF16)
        part = _dot(a, wd_ref[c0:c1, :])
        acc = part if acc is None else acc + part
    return x + _rmsnorm(acc, post_ref[...])


def _row_spec():
    return pl.BlockSpec((ROW_TILE, D_MODEL), lambda i: (i, 0))


def _ffn_specs():
    return [_const_spec((1, D_MODEL)), _const_spec((D_MODEL, D_FF)),
            _const_spec((D_MODEL, D_FF)), _const_spec((D_FF, D_MODEL)),
            _const_spec((1, D_MODEL))]


def _layer0_tail_kernel(o_ref, sg_ref, x_ref, wo_ref, mpost_ref,
                        fpre_ref, wg_ref, wu_ref, wd_ref, fpost_ref,
                        spre_ref, wsin_ref, x2_ref, u_ref):
    m = _dot(o_ref[...] * sg_ref[...], wo_ref[...])
    x1 = x_ref[...] + _rmsnorm(m, mpost_ref[...])
    x2 = _ffn_body(x1, fpre_ref, wg_ref, wu_ref, wd_ref, fpost_ref)
    x2_ref[...] = x2
    u_ref[...] = _dot(_rmsnorm(x2, spre_ref[...]).astype(BF16), wsin_ref[...]).astype(BF16)


def _layer0_tail(o, sg, x, w_out, mix_post, ffn_w, s5_pre, s5_w_in):
    return pl.pallas_call(
        _layer0_tail_kernel,
        grid=(N_TOK // ROW_TILE,),
        in_specs=[_row_spec(), _row_spec(), _row_spec(),
                  _const_spec((D_ATT, D_MODEL)), _const_spec((1, D_MODEL)),
                  *_ffn_specs(),
                  _const_spec((1, D_MODEL)), _const_spec((D_MODEL, D_MODEL))],
        out_specs=[_row_spec(), _row_spec()],
        out_shape=[jax.ShapeDtypeStruct((N_TOK, D_MODEL), F32),
                   jax.ShapeDtypeStruct((N_TOK, D_MODEL), BF16)],
        compiler_params=_params("arbitrary"),
        name="layer0_tail",
    )(o, sg, x, w_out, mix_post, *ffn_w, s5_pre, s5_w_in)


def _layer1_tail_kernel(y_ref, x_ref, wglu_ref, wo_ref, mpost_ref,
                        fpre_ref, wg_ref, wu_ref, wd_ref, fpost_ref, out_ref):
    y = y_ref[...]
    gated = (y.astype(F32) * _sigmoid(_dot(y, wglu_ref[...]))).astype(BF16)
    x3 = x_ref[...] + _rmsnorm(_dot(gated, wo_ref[...]), mpost_ref[...])
    out_ref[...] = _ffn_body(x3, fpre_ref, wg_ref, wu_ref, wd_ref, fpost_ref)


def _layer1_tail(y, x, w_glu, w_out, mix_post, ffn_w):
    return pl.pallas_call(
        _layer1_tail_kernel,
        grid=(N_TOK // ROW_TILE,),
        in_specs=[_row_spec(), _row_spec(),
                  _const_spec((D_MODEL, D_MODEL)), _const_spec((D_MODEL, D_MODEL)),
                  _const_spec((1, D_MODEL)), *_ffn_specs()],
        out_specs=_row_spec(),
        out_shape=jax.ShapeDtypeStruct((N_TOK, D_MODEL), F32),
        compiler_params=_params("arbitrary"),
        name="layer1_tail",
    )(y, x, w_glu, w_out, mix_post, *ffn_w)


def _s5_disc_kernel(ldt_ref, lre_ref, lim_ref, bre_ref, bim_ref,
                    are_ref, aim_ref, bbre_ref, bbim_ref):
    dt = jnp.exp(ldt_ref[...])
    lam_re = lre_ref[...]
    lam_im = lim_ref[...]
    mag = jnp.exp(lam_re * dt)
    a_re = mag * jnp.cos(lam_im * dt)
    a_im = mag * jnp.sin(lam_im * dt)
    den = lam_re * lam_re + lam_im * lam_im
    n_re = a_re - 1.0
    z_re = (n_re * lam_re + a_im * lam_im) / den
    z_im = (a_im * lam_re - n_re * lam_im) / den
    are_ref[...] = a_re
    aim_ref[...] = a_im
    b_re = bre_ref[...]
    b_im = bim_ref[...]
    bbre_ref[...] = z_re * b_re - z_im * b_im
    bbim_ref[...] = z_re * b_im + z_im * b_re


def _s5_discretise(log_dt, lam_re, lam_im, b_re_t, b_im_t):
    gs = jax.ShapeDtypeStruct((N_GROUPS, 1, STATE), F32)
    gcs = jax.ShapeDtypeStruct((N_GROUPS, SSM_GROUP, STATE), F32)
    return pl.pallas_call(
        _s5_disc_kernel,
        out_shape=[gs, gs, gcs, gcs],
        name="s5_discretise",
    )(log_dt, lam_re, lam_im, b_re_t, b_im_t)


def _s5_scan_kernel(u_ref, p_ref, pt_ref, bre_ref, bim_ref, cre_ref, cim_ref,
                    are_ref, aim_ref, d_ref, y_ref, xr_ref, xi_ref, sr_ref, si_ref):
    @pl.when(pl.program_id(0) == 0)
    def _():
        sr_ref[...] = jnp.zeros_like(sr_ref)
        si_ref[...] = jnp.zeros_like(si_ref)

    u_f32 = _dot(p_ref[...], u_ref[...].reshape(SCAN_ROWS, D_MODEL))
    u = u_f32.astype(BF16)
    ys = []
    for g in range(N_GROUP_BLOCKS):
        ch = slice(g * GB_CH, (g + 1) * GB_CH)
        xr_ref[g] = _dot(u[:, ch], bre_ref[g])
        xi_ref[g] = _dot(u[:, ch], bim_ref[g])
        a_re = jnp.broadcast_to(are_ref[g], (BATCH, GB_ST))
        a_im = jnp.broadcast_to(aim_ref[g], (BATCH, GB_ST))
        s_re = sr_ref[g]
        s_im = si_ref[g]
        for t in range(SCAN_STEPS):
            rows = slice(t * BATCH, (t + 1) * BATCH)
            n_re = a_re * s_re - a_im * s_im + xr_ref[g, rows, :]
            n_im = a_re * s_im + a_im * s_re + xi_ref[g, rows, :]
            xr_ref[g, rows, :] = n_re
            xi_ref[g, rows, :] = n_im
            s_re, s_im = n_re, n_im
        sr_ref[g] = s_re
        si_ref[g] = s_im
        y = (_dot(xr_ref[g].astype(BF16), cre_ref[g])
             - _dot(xi_ref[g].astype(BF16), cim_ref[g]))
        y = y + d_ref[:, ch] * u_f32[:, ch]
        y = 0.5 * y * (1.0 + jnp.tanh(math.sqrt(2.0 / math.pi) * (y + 0.044715 * (y * y * y))))
        ys.append(y.astype(BF16))
    y_bm = _dot(pt_ref[...], jnp.concatenate(ys, axis=1)).astype(BF16)
    y_ref[...] = y_bm.reshape(BATCH, SCAN_STEPS, D_MODEL)


def _time_major_permutation():
    r = np.arange(SCAN_ROWS)
    src = (r % BATCH) * SCAN_STEPS + r // BATCH
    return (src[:, None] == np.arange(SCAN_ROWS)[None, :]).astype(np.float32)


def _s5_scan(u, b_re, b_im, c_re, c_im, a_re, a_im, d_skip):
    perm = _time_major_permutation()
    tile = pl.BlockSpec((BATCH, SCAN_STEPS, D_MODEL), lambda t: (0, t, 0))
    sq = (SCAN_ROWS, SCAN_ROWS)
    b_shape = (N_GROUP_BLOCKS, GB_CH, GB_ST)
    c_shape = (N_GROUP_BLOCKS, GB_ST, GB_CH)
    a_shape = (N_GROUP_BLOCKS, 1, GB_ST)
    return pl.pallas_call(
        _s5_scan_kernel,
        grid=(SEQ // SCAN_STEPS,),
        in_specs=[tile, _const_spec(sq), _const_spec(sq),
                  _const_spec(b_shape), _const_spec(b_shape),
                  _const_spec(c_shape), _const_spec(c_shape),
                  _const_spec(a_shape), _const_spec(a_shape),
                  _const_spec((1, D_MODEL))],
        out_specs=tile,
        out_shape=jax.ShapeDtypeStruct((BATCH, SEQ, D_MODEL), BF16),
        scratch_shapes=[pltpu.VMEM((N_GROUP_BLOCKS, SCAN_ROWS, GB_ST), F32),
                        pltpu.VMEM((N_GROUP_BLOCKS, SCAN_ROWS, GB_ST), F32),
                        pltpu.VMEM((N_GROUP_BLOCKS, BATCH, GB_ST), F32),
                        pltpu.VMEM((N_GROUP_BLOCKS, BATCH, GB_ST), F32)],
        compiler_params=_params("arbitrary"),
        name="s5_scan",
    )(u.reshape(BATCH, SEQ, D_MODEL), jnp.asarray(perm, BF16), jnp.asarray(perm.T, BF16),
      b_re, b_im, c_re, c_im, a_re, a_im, d_skip)


def _block_diag_in(bb_t):
    x = bb_t.reshape(N_GROUP_BLOCKS, GROUP_BLOCK, SSM_GROUP, STATE)
    eye = jnp.eye(GROUP_BLOCK, dtype=bb_t.dtype)
    x = jnp.einsum('klcp,lm->klcmp', x, eye)
    return x.reshape(N_GROUP_BLOCKS, GB_CH, GB_ST).astype(BF16)


def _block_diag_out(c):
    x = c.reshape(N_GROUP_BLOCKS, GROUP_BLOCK, SSM_GROUP, STATE)
    eye = jnp.eye(GROUP_BLOCK, dtype=c.dtype)
    x = jnp.einsum('klcp,lm->klpmc', x, eye)
    return x.reshape(N_GROUP_BLOCKS, GB_ST, GB_CH).astype(BF16)


def kernel(x, fox_w_in, fox_b_f, fox_q_gain, fox_k_gain, fox_w_out, s5_w_in, s5_log_dt, s5_lam_re, s5_lam_im, s5_b_re, s5_b_im, s5_c_re, s5_c_im, s5_d, s5_w_glu, s5_w_out, mix_pre_gain, mix_post_gain, ffn_pre_gain, ffn_post_gain, ffn_w_gate, ffn_w_up, ffn_w_down):
    row = lambda a: a.reshape(1, -1).astype(F32)
    x0 = x.reshape(N_TOK, D_MODEL)

    def ffn_weights(i):
        return (row(ffn_pre_gain[i]), ffn_w_gate[i].astype(BF16), ffn_w_up[i].astype(BF16),
                ffn_w_down[i].astype(BF16), row(ffn_post_gain[i]))

    w_in = fox_w_in[0]
    w_main = w_in[:, :4 * D_ATT].astype(BF16)
    w_f = jnp.pad(w_in[:, 4 * D_ATT:], ((0, 0), (0, LANES - N_HEADS))).astype(BF16)
    b_f = jnp.pad(fox_b_f[0], (0, LANES - N_HEADS)).reshape(1, LANES).astype(F32)
    reps = 256 // HEAD_DIM
    q_gain = row(jnp.tile(fox_q_gain[0] * (HEAD_DIM ** -0.5 * LOG2E), reps))
    k_gain = row(jnp.tile(fox_k_gain[0], reps))
    head_of = jnp.arange(256) // HEAD_DIM
    e_blk = (head_of[:, None] == head_of[None, :]).astype(BF16)

    qt, k, vt, sg, c = _fox_in(x0, row(mix_pre_gain[0]), w_main, w_f, b_f, q_gain, k_gain, e_blk)
    o = _attention(qt, k, vt, c)
    x2, u = _layer0_tail(o, sg, x0, fox_w_out[0].astype(BF16), row(mix_post_gain[0]),
                         ffn_weights(0), row(mix_pre_gain[1]), s5_w_in[0].astype(BF16))

    gps = (N_GROUPS, 1, STATE)
    a_re, a_im, bb_re, bb_im = _s5_discretise(
        s5_log_dt[0].reshape(N_GROUPS, 1, 1).astype(F32),
        s5_lam_re[0].reshape(gps).astype(F32), s5_lam_im[0].reshape(gps).astype(F32),
        s5_b_re[0].transpose(0, 2, 1).astype(F32), s5_b_im[0].transpose(0, 2, 1).astype(F32))
    y = _s5_scan(u, _block_diag_in(bb_re), _block_diag_in(bb_im),
                 _block_diag_out(s5_c_re[0]), _block_diag_out(s5_c_im[0]),
                 a_re.reshape(N_GROUP_BLOCKS, 1, GB_ST), a_im.reshape(N_GROUP_BLOCKS, 1, GB_ST),
                 row(s5_d[0]))
    x4 = _layer1_tail(y.reshape(N_TOK, D_MODEL), x2, s5_w_glu[0].astype(BF16),
                      s5_w_out[0].astype(BF16), row(mix_post_gain[1]), ffn_weights(1))
    return x4.reshape(BATCH, SEQ, D_MODEL)
```

```python
import math

import numpy as np

import jax
import jax.numpy as jnp
from jax import lax
from jax.experimental import pallas as pl
from jax.experimental.pallas import tpu as pltpu

F32 = jnp.float32
BF16 = jnp.bfloat16

D_MODEL = 1024
BATCH = 16
SEQ = 2048
N_TOK = BATCH * SEQ
N_HEADS = 16
HEAD_DIM = 64
D_ATT = N_HEADS * HEAD_DIM
SSM_GROUP = 16
N_GROUPS = 64
STATE = 64
D_FF = 2816
EPS = 1e-6

LANES = 128
ROW_TILE = 512
TILES_PER_SEQ = SEQ // ROW_TILE
SUB_TILES = (slice(0, ROW_TILE // 2), slice(ROW_TILE // 2, ROW_TILE))
ATT_TQ = 256
ATT_LOOKAHEAD = 2
ATT_ONES_ROWS = 16
LOG2E = math.log2(math.e)
HEAD_PAIRS = N_HEADS // 2
SCAN_STEPS = 32
SCAN_ROWS = SCAN_STEPS * BATCH
PERM_STEPS = 16
PERM_ROWS = PERM_STEPS * BATCH
GROUP_BLOCK = 16
N_GROUP_BLOCKS = N_GROUPS // GROUP_BLOCK
GB_CH = GROUP_BLOCK * SSM_GROUP
GB_ST = GROUP_BLOCK * STATE
MXU_DIM = 256
_FF_SPLIT = (D_FF // MXU_DIM // 2) * MXU_DIM
FF_CHUNKS = ((0, _FF_SPLIT), (_FF_SPLIT, D_FF))
VMEM_LIMIT = 56 * 1024 * 1024


def _const_spec(shape):
    nd = len(shape)
    return pl.BlockSpec(shape, lambda *_: (0,) * nd, pipeline_mode=pl.Buffered(1))


def _rmsnorm(xf, gain):
    return xf * lax.rsqrt(jnp.mean(xf * xf, axis=-1, keepdims=True) + EPS) * gain


def _dot(a, b):
    return jnp.dot(a, b, preferred_element_type=F32)


def _sigmoid(x):
    return 1.0 / (1.0 + jnp.exp(-x))


def _params(*sem):
    return pltpu.CompilerParams(dimension_semantics=sem, vmem_limit_bytes=VMEM_LIMIT)


def _fox_in_kernel(x_ref, g_ref, w_ref, wf_ref, bf_ref, qg_ref, kg_ref, e_ref,
                   qt_ref, k_ref, vt_ref, sg_ref, c_ref, carry_ref):
    i = pl.program_id(0)
    hs = [_rmsnorm(x_ref[r, :], g_ref[...]).astype(BF16) for r in SUB_TILES]

    z = jnp.concatenate([_dot(h, wf_ref[...]) for h in hs], axis=0) + bf_ref[...]
    log_f = jnp.minimum(z, 0.0) - jnp.log(1.0 + jnp.exp(-jnp.abs(z)))
    rows = lax.broadcasted_iota(jnp.int32, (ROW_TILE, ROW_TILE), 0)
    cols = lax.broadcasted_iota(jnp.int32, (ROW_TILE, ROW_TILE), 1)
    tril = (rows >= cols).astype(BF16)
    hi = log_f.astype(BF16)
    r1 = log_f - hi.astype(F32)
    mid = r1.astype(BF16)
    lo3 = (r1 - mid.astype(F32)).astype(BF16)
    cum = _dot(tril, hi) + _dot(tril, mid) + _dot(tril, lo3)

    def head_norm(raw, gain):
        out = []
        for lo in range(0, D_ATT, 256):
            blk = raw[:, lo:lo + 256]
            ssq = _dot((blk * blk).astype(BF16), e_ref[...])
            out.append(blk * lax.rsqrt(ssq * (1.0 / HEAD_DIM) + EPS) * gain[...])
        return out

    q_raws = [_dot(h, w_ref[:, :D_ATT]) for h in hs]
    k_raws = [_dot(h, w_ref[:, D_ATT:2 * D_ATT]) for h in hs]
    for r, q_raw in zip(SUB_TILES, q_raws):
        for n, blk in enumerate(head_norm(q_raw, qg_ref)):
            qt_ref[n * 256:(n + 1) * 256, r] = blk.T.astype(BF16)
    for r, h in zip(SUB_TILES, hs):
        vt_ref[:, r] = _dot(h, w_ref[:, 2 * D_ATT:3 * D_ATT]).T.astype(BF16)
    for r, k_raw in zip(SUB_TILES, k_raws):
        for n, blk in enumerate(head_norm(k_raw, kg_ref)):
            k_ref[r, n * 256:(n + 1) * 256] = blk.astype(BF16)
    for r, h in zip(SUB_TILES, hs):
        sg_ref[r, :] = _sigmoid(_dot(h, w_ref[:, 3 * D_ATT:4 * D_ATT])).astype(BF16)

    @pl.when(i % TILES_PER_SEQ == 0)
    def _():
        carry_ref[...] = jnp.zeros_like(carry_ref)

    cum = cum + carry_ref[...]
    c_ref[...] = cum * LOG2E
    carry_ref[...] = cum[ROW_TILE - 1:ROW_TILE, :]


def _fox_in(x, pre_gain, w_main, w_f, b_f, q_gain, k_gain, e_blk):
    row_spec = pl.BlockSpec((ROW_TILE, D_MODEL), lambda i: (i, 0))
    col_spec = pl.BlockSpec((D_ATT, ROW_TILE), lambda i: (0, i))
    act = jax.ShapeDtypeStruct((N_TOK, D_ATT), BF16)
    act_t = jax.ShapeDtypeStruct((D_ATT, N_TOK), BF16)
    return pl.pallas_call(
        _fox_in_kernel,
        grid=(N_TOK // ROW_TILE,),
        in_specs=[
            row_spec,
            _const_spec((1, D_MODEL)),
            _layer_spec((D_MODEL, 4 * D_ATT), 0),
            _const_spec((D_MODEL, LANES)),
            _const_spec((1, LANES)),
            _const_spec((1, 256)),
            _const_spec((1, 256)),
            _const_spec((256, 256)),
        ],
        out_specs=[col_spec, row_spec, col_spec, row_spec,
                   pl.BlockSpec((ROW_TILE, LANES), lambda i: (i, 0))],
        out_shape=[act_t, act, act_t, act, jax.ShapeDtypeStruct((N_TOK, LANES), F32)],
        scratch_shapes=[pltpu.VMEM((1, LANES), F32)],
        compiler_params=_params("arbitrary"),
        name="fox_in",
    )(x, pre_gain, w_main, w_f, b_f, q_gain, k_gain, e_blk)


def _attn_kernel(qt_ref, k_ref, vt_ref, c_ref, o_ref):
    pair = pl.program_id(1)
    k = k_ref[...]
    qt = qt_ref[...].astype(F32)
    low_rows = lax.broadcasted_iota(jnp.int32, (LANES, SEQ), 0) < HEAD_DIM
    qmt = (jnp.where(low_rows, qt, 0.0).astype(BF16), jnp.where(low_rows, 0.0, qt).astype(BF16))
    ones = jnp.ones((ATT_ONES_ROWS, SEQ), BF16)
    vmt = tuple(jnp.concatenate([vt_ref[hh * HEAD_DIM:(hh + 1) * HEAD_DIM, :], ones], axis=0)
                for hh in range(2))
    c_all = c_ref[...]
    lane = lax.broadcasted_iota(jnp.int32, (SEQ, LANES), 1)
    cb = tuple(
        jnp.broadcast_to(
            jnp.sum(jnp.where(lane == 2 * pair + hh, c_all, 0.0), axis=-1, keepdims=True),
            (SEQ, ATT_TQ))
        for hh in range(2))
    key = lax.broadcasted_iota(jnp.int32, (ATT_TQ, ATT_TQ), 0)
    qry = lax.broadcasted_iota(jnp.int32, (ATT_TQ, ATT_TQ), 1)
    causal = key <= qry

    def scores(qi, hh):
        k0 = qi * ATT_TQ
        qcols = qmt[hh][:, k0:k0 + ATT_TQ]
        s_d = _dot(k[k0:k0 + ATT_TQ], qcols)
        s_o = _dot(k[:k0], qcols) if qi else None
        return s_d, s_o

    def attend(qi, hh, s_d, s_o):
        k0 = qi * ATT_TQ
        s_d = jnp.where(causal, s_d - cb[hh][k0:k0 + ATT_TQ], -1e30)
        m = jnp.max(s_d, axis=0, keepdims=True)
        if qi:
            s_o = s_o - cb[hh][:k0]
            m = jnp.maximum(m, jnp.max(s_o, axis=0, keepdims=True))
        acc = _dot(vmt[hh][:, k0:k0 + ATT_TQ], jnp.exp2(s_d - m).astype(BF16))
        if qi:
            acc = acc + _dot(vmt[hh][:, :k0], jnp.exp2(s_o - m).astype(BF16))
        return acc[:HEAD_DIM] / acc[HEAD_DIM:HEAD_DIM + 1]

    items = [(qi, hh) for qi in range(SEQ // ATT_TQ) for hh in range(2)]
    ahead = [scores(*it) for it in items[:ATT_LOOKAHEAD]]
    outs = []
    for n, (qi, hh) in enumerate(items):
        current = ahead.pop(0)
        if n + ATT_LOOKAHEAD < len(items):
            ahead.append(scores(*items[n + ATT_LOOKAHEAD]))
        outs.append(attend(qi, hh, *current))
        if hh == 1:
            k0 = qi * ATT_TQ
            o_ref[k0:k0 + ATT_TQ, :] = jnp.concatenate(outs, axis=0).T.astype(BF16)
            outs = []


def _attention(qt, k, vt, c):
    seq_spec = pl.BlockSpec((SEQ, LANES), lambda b, j: (b, j))
    seq_t_spec = pl.BlockSpec((LANES, SEQ), lambda b, j: (j, b))
    return pl.pallas_call(
        _attn_kernel,
        grid=(BATCH, HEAD_PAIRS),
        in_specs=[seq_t_spec, seq_spec, seq_t_spec,
                  pl.BlockSpec((SEQ, LANES), lambda b, j: (b, 0))],
        out_specs=seq_spec,
        out_shape=jax.ShapeDtypeStruct((N_TOK, D_ATT), BF16),
        compiler_params=_params("arbitrary", "arbitrary"),
        name="fox_attention",
    )(qt, k, vt, c)


def _ffn_body(xs, pre_ref, wg_ref, wu_ref, wd_ref, post_ref):
    hs = [_rmsnorm(x, pre_ref[...]).astype(BF16) for x in xs]
    accs = [None] * len(xs)
    for c0, c1 in FF_CHUNKS:
        for n, h in enumerate(hs):
            g = _dot(h, wg_ref[:, c0:c1])
            u = _dot(h, wu_ref[:, c0:c1])
            a = (g * _sigmoid(g) * u).astype(BF16)
            part = _dot(a, wd_ref[c0:c1, :])
            accs[n] = part if accs[n] is None else accs[n] + part
    return [x + _rmsnorm(acc, post_ref[...]) for x, acc in zip(xs, accs)]


def _row_spec():
    return pl.BlockSpec((ROW_TILE, D_MODEL), lambda i: (i, 0))


def _layer_spec(shape, layer):
    nd = len(shape)
    return pl.BlockSpec((None,) + shape, lambda *_: (layer,) + (0,) * nd,
                        pipeline_mode=pl.Buffered(1))


def _ffn_specs(layer):
    return [_const_spec((1, D_MODEL)), _layer_spec((D_MODEL, D_FF), layer),
            _layer_spec((D_MODEL, D_FF), layer), _layer_spec((D_FF, D_MODEL), layer),
            _const_spec((1, D_MODEL))]


def _layer0_tail_kernel(o_ref, sg_ref, x_ref, wo_ref, mpost_ref,
                        fpre_ref, wg_ref, wu_ref, wd_ref, fpost_ref,
                        spre_ref, wsin_ref, x2_ref, u_ref):
    ms = [_dot(o_ref[r, :] * sg_ref[r, :], wo_ref[...]) for r in SUB_TILES]
    x1s = [x_ref[r, :] + _rmsnorm(m, mpost_ref[...]) for r, m in zip(SUB_TILES, ms)]
    x2s = _ffn_body(x1s, fpre_ref, wg_ref, wu_ref, wd_ref, fpost_ref)
    for r, x2 in zip(SUB_TILES, x2s):
        x2_ref[r, :] = x2
        u_ref[r, :] = _dot(_rmsnorm(x2, spre_ref[...]).astype(BF16), wsin_ref[...]).astype(BF16)


def _layer0_tail(o, sg, x, w_out, mix_post, ffn_w, s5_pre, s5_w_in):
    return pl.pallas_call(
        _layer0_tail_kernel,
        grid=(N_TOK // ROW_TILE,),
        in_specs=[_row_spec(), _row_spec(), _row_spec(),
                  _const_spec((D_ATT, D_MODEL)), _const_spec((1, D_MODEL)),
                  *_ffn_specs(0),
                  _const_spec((1, D_MODEL)), _const_spec((D_MODEL, D_MODEL))],
        out_specs=[_row_spec(), _row_spec()],
        out_shape=[jax.ShapeDtypeStruct((N_TOK, D_MODEL), F32),
                   jax.ShapeDtypeStruct((N_TOK, D_MODEL), BF16)],
        compiler_params=_params("arbitrary"),
        name="layer0_tail",
    )(o, sg, x, w_out, mix_post, *ffn_w, s5_pre, s5_w_in)


def _layer1_tail_kernel(y_ref, x_ref, wglu_ref, wo_ref, mpost_ref,
                        fpre_ref, wg_ref, wu_ref, wd_ref, fpost_ref, out_ref):
    ms = []
    for r in SUB_TILES:
        y = y_ref[r, :]
        gated = (y.astype(F32) * _sigmoid(_dot(y, wglu_ref[...]))).astype(BF16)
        ms.append(_dot(gated, wo_ref[...]))
    x3s = [x_ref[r, :] + _rmsnorm(m, mpost_ref[...]) for r, m in zip(SUB_TILES, ms)]
    outs = _ffn_body(x3s, fpre_ref, wg_ref, wu_ref, wd_ref, fpost_ref)
    for r, out in zip(SUB_TILES, outs):
        out_ref[r, :] = out


def _layer1_tail(y, x, w_glu, w_out, mix_post, ffn_w):
    return pl.pallas_call(
        _layer1_tail_kernel,
        grid=(N_TOK // ROW_TILE,),
        in_specs=[_row_spec(), _row_spec(),
                  _const_spec((D_MODEL, D_MODEL)), _const_spec((D_MODEL, D_MODEL)),
                  _const_spec((1, D_MODEL)), *_ffn_specs(1)],
        out_specs=_row_spec(),
        out_shape=jax.ShapeDtypeStruct((N_TOK, D_MODEL), F32),
        compiler_params=_params("arbitrary"),
        name="layer1_tail",
    )(y, x, w_glu, w_out, mix_post, *ffn_w)


def _s5_disc_kernel(ldt_ref, lre_ref, lim_ref, bre_ref, bim_ref,
                    are_ref, aim_ref, bbre_ref, bbim_ref):
    dt = jnp.exp(ldt_ref[...])
    lam_re = lre_ref[...]
    lam_im = lim_ref[...]
    mag = jnp.exp(lam_re * dt)
    a_re = mag * jnp.cos(lam_im * dt)
    a_im = mag * jnp.sin(lam_im * dt)
    den = lam_re * lam_re + lam_im * lam_im
    n_re = a_re - 1.0
    z_re = (n_re * lam_re + a_im * lam_im) / den
    z_im = (a_im * lam_re - n_re * lam_im) / den
    are_ref[...] = a_re
    aim_ref[...] = a_im
    b_re = bre_ref[...]
    b_im = bim_ref[...]
    bbre_ref[...] = z_re * b_re - z_im * b_im
    bbim_ref[...] = z_re * b_im + z_im * b_re


def _s5_discretise(log_dt, lam_re, lam_im, b_re_t, b_im_t):
    gs = jax.ShapeDtypeStruct((N_GROUPS, 1, STATE), F32)
    gcs = jax.ShapeDtypeStruct((N_GROUPS, SSM_GROUP, STATE), F32)
    return pl.pallas_call(
        _s5_disc_kernel,
        out_shape=[gs, gs, gcs, gcs],
        name="s5_discretise",
    )(log_dt, lam_re, lam_im, b_re_t, b_im_t)


def _s5_scan_kernel(u_ref, p_ref, pt_ref, bre_ref, bim_ref, cre_ref, cim_ref,
                    are_ref, aim_ref, d_ref, y_ref, xr_ref, xi_ref, sr_ref, si_ref):
    @pl.when(pl.program_id(0) == 0)
    def _():
        sr_ref[...] = jnp.zeros_like(sr_ref)
        si_ref[...] = jnp.zeros_like(si_ref)

    u_f32 = jnp.concatenate(
        [_dot(p_ref[...], u_ref[:, t0:t0 + PERM_STEPS, :].reshape(PERM_ROWS, D_MODEL))
         for t0 in range(0, SCAN_STEPS, PERM_STEPS)], axis=0)
    u = u_f32.astype(BF16)

    def channels(g):
        return slice(g * GB_CH, (g + 1) * GB_CH)

    def drive(g):
        xr_ref[g] = _dot(u[:, channels(g)], bre_ref[g])
        xi_ref[g] = _dot(u[:, channels(g)], bim_ref[g])

    def recur(g):
        a_re = jnp.broadcast_to(are_ref[g], (BATCH, GB_ST))
        a_im = jnp.broadcast_to(aim_ref[g], (BATCH, GB_ST))
        s_re = sr_ref[g]
        s_im = si_ref[g]
        for t in range(SCAN_STEPS):
            rows = slice(t * BATCH, (t + 1) * BATCH)
            n_re = a_re * s_re - a_im * s_im + xr_ref[g, rows, :]
            n_im = a_re * s_im + a_im * s_re + xi_ref[g, rows, :]
            xr_ref[g, rows, :] = n_re
            xi_ref[g, rows, :] = n_im
            s_re, s_im = n_re, n_im
        sr_ref[g] = s_re
        si_ref[g] = s_im

    def readout(g):
        ch = channels(g)
        y = (_dot(xr_ref[g].astype(BF16), cre_ref[g])
             - _dot(xi_ref[g].astype(BF16), cim_ref[g]))
        y = y + d_ref[:, ch] * u_f32[:, ch]
        y = 0.5 * y * (1.0 + jnp.tanh(math.sqrt(2.0 / math.pi) * (y + 0.044715 * (y * y * y))))
        y = y.astype(BF16)
        for t0 in range(0, SCAN_STEPS, PERM_STEPS):
            y_bm = _dot(pt_ref[...], y[t0 * BATCH:(t0 + PERM_STEPS) * BATCH]).astype(BF16)
            y_ref[:, t0:t0 + PERM_STEPS, ch] = y_bm.reshape(BATCH, PERM_STEPS, GB_CH)

    drive(0)
    for g in range(N_GROUP_BLOCKS):
        if g + 1 < N_GROUP_BLOCKS:
            drive(g + 1)
        recur(g)
        readout(g)


def _time_major_permutation():
    r = np.arange(PERM_ROWS)
    src = (r % BATCH) * PERM_STEPS + r // BATCH
    return (src[:, None] == np.arange(PERM_ROWS)[None, :]).astype(np.float32)


def _s5_scan(u, b_re, b_im, c_re, c_im, a_re, a_im, d_skip):
    perm = _time_major_permutation()
    tile = pl.BlockSpec((BATCH, SCAN_STEPS, D_MODEL), lambda t: (0, t, 0))
    sq = (PERM_ROWS, PERM_ROWS)
    b_shape = (N_GROUP_BLOCKS, GB_CH, GB_ST)
    c_shape = (N_GROUP_BLOCKS, GB_ST, GB_CH)
    a_shape = (N_GROUP_BLOCKS, 1, GB_ST)
    return pl.pallas_call(
        _s5_scan_kernel,
        grid=(SEQ // SCAN_STEPS,),
        in_specs=[tile, _const_spec(sq), _const_spec(sq),
                  _const_spec(b_shape), _const_spec(b_shape),
                  _const_spec(c_shape), _const_spec(c_shape),
                  _const_spec(a_shape), _const_spec(a_shape),
                  _const_spec((1, D_MODEL))],
        out_specs=tile,
        out_shape=jax.ShapeDtypeStruct((BATCH, SEQ, D_MODEL), BF16),
        scratch_shapes=[pltpu.VMEM((N_GROUP_BLOCKS, SCAN_ROWS, GB_ST), F32),
                        pltpu.VMEM((N_GROUP_BLOCKS, SCAN_ROWS, GB_ST), F32),
                        pltpu.VMEM((N_GROUP_BLOCKS, BATCH, GB_ST), F32),
                        pltpu.VMEM((N_GROUP_BLOCKS, BATCH, GB_ST), F32)],
        compiler_params=_params("arbitrary"),
        name="s5_scan",
    )(u.reshape(BATCH, SEQ, D_MODEL), jnp.asarray(perm, BF16), jnp.asarray(perm.T, BF16),
      b_re, b_im, c_re, c_im, a_re, a_im, d_skip)


def _block_diag_in(bb_t):
    x = bb_t.reshape(N_GROUP_BLOCKS, GROUP_BLOCK, SSM_GROUP, STATE)
    eye = jnp.eye(GROUP_BLOCK, dtype=bb_t.dtype)
    x = jnp.einsum('klcp,lm->klcmp', x, eye)
    return x.reshape(N_GROUP_BLOCKS, GB_CH, GB_ST).astype(BF16)


def _block_diag_out(c):
    x = c.reshape(N_GROUP_BLOCKS, GROUP_BLOCK, SSM_GROUP, STATE)
    eye = jnp.eye(GROUP_BLOCK, dtype=c.dtype)
    x = jnp.einsum('klcp,lm->klpmc', x, eye)
    return x.reshape(N_GROUP_BLOCKS, GB_ST, GB_CH).astype(BF16)


def kernel(x, fox_w_in, fox_b_f, fox_q_gain, fox_k_gain, fox_w_out, s5_w_in, s5_log_dt, s5_lam_re, s5_lam_im, s5_b_re, s5_b_im, s5_c_re, s5_c_im, s5_d, s5_w_glu, s5_w_out, mix_pre_gain, mix_post_gain, ffn_pre_gain, ffn_post_gain, ffn_w_gate, ffn_w_up, ffn_w_down):
    row = lambda a: a.reshape(1, -1).astype(F32)
    x0 = x.reshape(N_TOK, D_MODEL)

    w_gate, w_up, w_down = (w.astype(BF16) for w in (ffn_w_gate, ffn_w_up, ffn_w_down))

    def ffn_weights(i):
        return (row(ffn_pre_gain[i]), w_gate, w_up, w_down, row(ffn_post_gain[i]))

    w_main = fox_w_in.astype(BF16)
    w_f = jnp.pad(fox_w_in[0][:, 4 * D_ATT:], ((0, 0), (0, LANES - N_HEADS))).astype(BF16)
    b_f = jnp.pad(fox_b_f[0], (0, LANES - N_HEADS)).reshape(1, LANES).astype(F32)
    reps = 256 // HEAD_DIM
    q_gain = row(jnp.tile(fox_q_gain[0] * (HEAD_DIM ** -0.5 * LOG2E), reps))
    k_gain = row(jnp.tile(fox_k_gain[0], reps))
    head_of = jnp.arange(256) // HEAD_DIM
    e_blk = (head_of[:, None] == head_of[None, :]).astype(BF16)

    qt, k, vt, sg, c = _fox_in(x0, row(mix_pre_gain[0]), w_main, w_f, b_f, q_gain, k_gain, e_blk)
    o = _attention(qt, k, vt, c)
    x2, u = _layer0_tail(o, sg, x0, fox_w_out[0].astype(BF16), row(mix_post_gain[0]),
                         ffn_weights(0), row(mix_pre_gain[1]), s5_w_in[0].astype(BF16))

    gps = (N_GROUPS, 1, STATE)
    a_re, a_im, bb_re, bb_im = _s5_discretise(
        s5_log_dt[0].reshape(N_GROUPS, 1, 1).astype(F32),
        s5_lam_re[0].reshape(gps).astype(F32), s5_lam_im[0].reshape(gps).astype(F32),
        s5_b_re[0].transpose(0, 2, 1).astype(F32), s5_b_im[0].transpose(0, 2, 1).astype(F32))
    y = _s5_scan(u, _block_diag_in(bb_re), _block_diag_in(bb_im),
                 _block_diag_out(s5_c_re[0]), _block_diag_out(s5_c_im[0]),
                 a_re.reshape(N_GROUP_BLOCKS, 1, GB_ST), a_im.reshape(N_GROUP_BLOCKS, 1, GB_ST),
                 row(s5_d[0]))
    x4 = _layer1_tail(y.reshape(N_TOK, D_MODEL), x2, s5_w_glu[0].astype(BF16),
                      s5_w_out[0].astype(BF16), row(mix_post_gain[1]), ffn_weights(1))
    return x4.reshape(BATCH, SEQ, D_MODEL)
```

```python
import math

import jax
import jax.numpy as jnp
from jax import lax
from jax.experimental import pallas as pl
from jax.experimental.pallas import tpu as pltpu

F32 = jnp.float32
BF16 = jnp.bfloat16

D_MODEL = 1024
BATCH = 16
SEQ = 2048
N_TOK = BATCH * SEQ
N_HEADS = 16
HEAD_DIM = 64
D_ATT = N_HEADS * HEAD_DIM
SSM_GROUP = 16
N_GROUPS = 64
STATE = 64
D_FF = 2816
EPS = 1e-6

LANES = 128
ROW_TILE = 512
TILES_PER_SEQ = SEQ // ROW_TILE
SUB_TILES = (slice(0, ROW_TILE // 2), slice(ROW_TILE // 2, ROW_TILE))
ATT_TQ = 256
ATT_LOOKAHEAD = 2
ATT_ONES_ROWS = 16
LOG2E = math.log2(math.e)
HEAD_PAIRS = N_HEADS // 2
SCAN_STEPS = 32
SCAN_ROWS = SCAN_STEPS * BATCH
PERM_STEPS = 16
PERM_ROWS = PERM_STEPS * BATCH
GROUP_BLOCK = 16
N_GROUP_BLOCKS = N_GROUPS // GROUP_BLOCK
GB_CH = GROUP_BLOCK * SSM_GROUP
GB_ST = GROUP_BLOCK * STATE
MXU_DIM = 256
_FF_SPLIT = (D_FF // MXU_DIM // 2) * MXU_DIM
FF_CHUNKS = ((0, _FF_SPLIT), (_FF_SPLIT, D_FF))
VMEM_LIMIT = 56 * 1024 * 1024


def _const_spec(shape):
    nd = len(shape)
    return pl.BlockSpec(shape, lambda *_: (0,) * nd, pipeline_mode=pl.Buffered(1))


def _rmsnorm(xf, gain):
    return xf * lax.rsqrt(jnp.mean(xf * xf, axis=-1, keepdims=True) + EPS) * gain


def _dot(a, b):
    return jnp.dot(a, b, preferred_element_type=F32)


def _dot_nt(a, b):
    return lax.dot_general(a, b, (((1,), (1,)), ((), ())), preferred_element_type=F32)


def _sigmoid(x):
    return 1.0 / (1.0 + jnp.exp(-x))


def _params(*sem):
    return pltpu.CompilerParams(dimension_semantics=sem, vmem_limit_bytes=VMEM_LIMIT)


def _fox_in_kernel(x_ref, g_ref, w_ref, wf_ref, bf_ref, qg_ref, kg_ref, e_ref,
                   qt_ref, k_ref, vt_ref, sg_ref, c_ref, carry_ref):
    i = pl.program_id(0)
    hs = [_rmsnorm(x_ref[r, :], g_ref[...]).astype(BF16) for r in SUB_TILES]

    z = jnp.concatenate([_dot(h, wf_ref[...]) for h in hs], axis=0) + bf_ref[...]
    log_f = jnp.minimum(z, 0.0) - jnp.log(1.0 + jnp.exp(-jnp.abs(z)))
    rows = lax.broadcasted_iota(jnp.int32, (ROW_TILE, ROW_TILE), 0)
    cols = lax.broadcasted_iota(jnp.int32, (ROW_TILE, ROW_TILE), 1)
    tril = (rows >= cols).astype(BF16)
    hi = log_f.astype(BF16).astype(F32)
    r1 = log_f - hi
    mid = r1.astype(BF16).astype(F32)
    lo3 = r1 - mid
    lane = lax.broadcasted_iota(jnp.int32, (ROW_TILE, LANES), 1)
    packed = jnp.where(lane < N_HEADS, hi,
                       jnp.where(lane < 2 * N_HEADS, pltpu.roll(mid, N_HEADS, 1),
                                 pltpu.roll(lo3, 2 * N_HEADS, 1)))
    sums = _dot(tril, packed.astype(BF16))
    cum = (sums + pltpu.roll(sums, LANES - N_HEADS, 1)
           + pltpu.roll(sums, LANES - 2 * N_HEADS, 1))

    def head_norm(raw, gain):
        out = []
        for lo in range(0, D_ATT, 256):
            blk = raw[:, lo:lo + 256]
            ssq = _dot((blk * blk).astype(BF16), e_ref[...])
            out.append(blk * lax.rsqrt(ssq * (1.0 / HEAD_DIM) + EPS) * gain[...])
        return out

    q_raws = [_dot(h, w_ref[:, :D_ATT]) for h in hs]
    k_raws = [_dot(h, w_ref[:, D_ATT:2 * D_ATT]) for h in hs]
    for r, q_raw in zip(SUB_TILES, q_raws):
        for n, blk in enumerate(head_norm(q_raw, qg_ref)):
            qt_ref[n * 256:(n + 1) * 256, r] = blk.T.astype(BF16)
    for r, h in zip(SUB_TILES, hs):
        vt_ref[:, r] = _dot(h, w_ref[:, 2 * D_ATT:3 * D_ATT]).T.astype(BF16)
    for r, k_raw in zip(SUB_TILES, k_raws):
        for n, blk in enumerate(head_norm(k_raw, kg_ref)):
            k_ref[r, n * 256:(n + 1) * 256] = blk.astype(BF16)
    for r, h in zip(SUB_TILES, hs):
        sg_ref[r, :] = _sigmoid(_dot(h, w_ref[:, 3 * D_ATT:4 * D_ATT])).astype(BF16)

    @pl.when(i % TILES_PER_SEQ == 0)
    def _():
        carry_ref[...] = jnp.zeros_like(carry_ref)

    cum = cum + carry_ref[...]
    c_ref[...] = cum * LOG2E
    carry_ref[...] = cum[ROW_TILE - 1:ROW_TILE, :]


def _fox_in(x, pre_gain, w_main, w_f, b_f, q_gain, k_gain, e_blk):
    row_spec = pl.BlockSpec((ROW_TILE, D_MODEL), lambda i: (i, 0))
    col_spec = pl.BlockSpec((D_ATT, ROW_TILE), lambda i: (0, i))
    act = jax.ShapeDtypeStruct((N_TOK, D_ATT), BF16)
    act_t = jax.ShapeDtypeStruct((D_ATT, N_TOK), BF16)
    return pl.pallas_call(
        _fox_in_kernel,
        grid=(N_TOK // ROW_TILE,),
        in_specs=[
            row_spec,
            _const_spec((1, D_MODEL)),
            _layer_spec((D_MODEL, 4 * D_ATT), 0),
            _const_spec((D_MODEL, LANES)),
            _const_spec((1, LANES)),
            _const_spec((1, 256)),
            _const_spec((1, 256)),
            _const_spec((256, 256)),
        ],
        out_specs=[col_spec, row_spec, col_spec, row_spec,
                   pl.BlockSpec((ROW_TILE, LANES), lambda i: (i, 0))],
        out_shape=[act_t, act, act_t, act, jax.ShapeDtypeStruct((N_TOK, LANES), F32)],
        scratch_shapes=[pltpu.VMEM((1, LANES), F32)],
        compiler_params=_params("arbitrary"),
        name="fox_in",
    )(x, pre_gain, w_main, w_f, b_f, q_gain, k_gain, e_blk)


def _attn_kernel(qt_ref, k_ref, vt_ref, c_ref, o_ref):
    pair = pl.program_id(1)
    k = k_ref[...]
    qt = qt_ref[...].astype(F32)
    low_rows = lax.broadcasted_iota(jnp.int32, (LANES, SEQ), 0) < HEAD_DIM
    qmt = (jnp.where(low_rows, qt, 0.0).astype(BF16), jnp.where(low_rows, 0.0, qt).astype(BF16))
    ones = jnp.ones((ATT_ONES_ROWS, SEQ), BF16)
    vmt = tuple(jnp.concatenate([vt_ref[hh * HEAD_DIM:(hh + 1) * HEAD_DIM, :], ones], axis=0)
                for hh in range(2))
    c_all = c_ref[...]
    lane = lax.broadcasted_iota(jnp.int32, (SEQ, LANES), 1)
    cb = tuple(
        jnp.broadcast_to(
            jnp.sum(jnp.where(lane == 2 * pair + hh, c_all, 0.0), axis=-1, keepdims=True),
            (SEQ, ATT_TQ))
        for hh in range(2))
    key = lax.broadcasted_iota(jnp.int32, (ATT_TQ, ATT_TQ), 0)
    qry = lax.broadcasted_iota(jnp.int32, (ATT_TQ, ATT_TQ), 1)
    causal = key <= qry

    def scores(qi, hh):
        k0 = qi * ATT_TQ
        qcols = qmt[hh][:, k0:k0 + ATT_TQ]
        s_d = _dot(k[k0:k0 + ATT_TQ], qcols)
        s_o = _dot(k[:k0], qcols) if qi else None
        return s_d, s_o

    def attend(qi, hh, s_d, s_o):
        k0 = qi * ATT_TQ
        s_d = jnp.where(causal, s_d - cb[hh][k0:k0 + ATT_TQ], -1e30)
        m = jnp.max(s_d, axis=0, keepdims=True)
        if qi:
            s_o = s_o - cb[hh][:k0]
            m = jnp.maximum(m, jnp.max(s_o, axis=0, keepdims=True))
        acc = _dot(vmt[hh][:, k0:k0 + ATT_TQ], jnp.exp2(s_d - m).astype(BF16))
        if qi:
            acc = acc + _dot(vmt[hh][:, :k0], jnp.exp2(s_o - m).astype(BF16))
        return acc[:HEAD_DIM] / acc[HEAD_DIM:HEAD_DIM + 1]

    items = [(qi, hh) for qi in range(SEQ // ATT_TQ) for hh in range(2)]
    ahead = [scores(*it) for it in items[:ATT_LOOKAHEAD]]
    outs = []
    for n, (qi, hh) in enumerate(items):
        current = ahead.pop(0)
        if n + ATT_LOOKAHEAD < len(items):
            ahead.append(scores(*items[n + ATT_LOOKAHEAD]))
        outs.append(attend(qi, hh, *current))
        if hh == 1:
            k0 = qi * ATT_TQ
            o_ref[k0:k0 + ATT_TQ, :] = jnp.concatenate(outs, axis=0).T.astype(BF16)
            outs = []


def _attention(qt, k, vt, c):
    seq_spec = pl.BlockSpec((SEQ, LANES), lambda b, j: (b, j))
    seq_t_spec = pl.BlockSpec((LANES, SEQ), lambda b, j: (j, b))
    return pl.pallas_call(
        _attn_kernel,
        grid=(BATCH, HEAD_PAIRS),
        in_specs=[seq_t_spec, seq_spec, seq_t_spec,
                  pl.BlockSpec((SEQ, LANES), lambda b, j: (b, 0))],
        out_specs=seq_spec,
        out_shape=jax.ShapeDtypeStruct((N_TOK, D_ATT), BF16),
        compiler_params=_params("arbitrary", "arbitrary"),
        name="fox_attention",
    )(qt, k, vt, c)


def _ffn_body(xs, pre_ref, wg_ref, wu_ref, wd_ref, post_ref):
    hs = [_rmsnorm(x, pre_ref[...]).astype(BF16) for x in xs]
    accs = [None] * len(xs)
    for c0, c1 in FF_CHUNKS:
        for n, h in enumerate(hs):
            g = _dot(h, wg_ref[:, c0:c1])
            u = _dot(h, wu_ref[:, c0:c1])
            a = (g * _sigmoid(g) * u).astype(BF16)
            part = _dot(a, wd_ref[c0:c1, :])
            accs[n] = part if accs[n] is None else accs[n] + part
    return [x + _rmsnorm(acc, post_ref[...]) for x, acc in zip(xs, accs)]


def _row_spec():
    return pl.BlockSpec((ROW_TILE, D_MODEL), lambda i: (i, 0))


def _layer_spec(shape, layer):
    nd = len(shape)
    return pl.BlockSpec((None,) + shape, lambda *_: (layer,) + (0,) * nd,
                        pipeline_mode=pl.Buffered(1))


def _ffn_specs(layer):
    return [_const_spec((1, D_MODEL)), _layer_spec((D_MODEL, D_FF), layer),
            _layer_spec((D_MODEL, D_FF), layer), _layer_spec((D_FF, D_MODEL), layer),
            _const_spec((1, D_MODEL))]


def _layer0_tail_kernel(o_ref, sg_ref, x_ref, wo_ref, mpost_ref,
                        fpre_ref, wg_ref, wu_ref, wd_ref, fpost_ref,
                        spre_ref, wsin_ref, x2_ref, u_ref):
    ms = [_dot(o_ref[r, :] * sg_ref[r, :], wo_ref[...]) for r in SUB_TILES]
    x1s = [x_ref[r, :] + _rmsnorm(m, mpost_ref[...]) for r, m in zip(SUB_TILES, ms)]
    x2s = _ffn_body(x1s, fpre_ref, wg_ref, wu_ref, wd_ref, fpost_ref)
    for r, x2 in zip(SUB_TILES, x2s):
        x2_ref[r, :] = x2
        u_ref[r, :] = _dot(_rmsnorm(x2, spre_ref[...]).astype(BF16), wsin_ref[...]).astype(BF16)


def _layer0_tail(o, sg, x, w_out, mix_post, ffn_w, s5_pre, s5_w_in):
    return pl.pallas_call(
        _layer0_tail_kernel,
        grid=(N_TOK // ROW_TILE,),
        in_specs=[_row_spec(), _row_spec(), _row_spec(),
                  _const_spec((D_ATT, D_MODEL)), _const_spec((1, D_MODEL)),
                  *_ffn_specs(0),
                  _const_spec((1, D_MODEL)), _const_spec((D_MODEL, D_MODEL))],
        out_specs=[_row_spec(), _row_spec()],
        out_shape=[jax.ShapeDtypeStruct((N_TOK, D_MODEL), F32),
                   jax.ShapeDtypeStruct((N_TOK, D_MODEL), BF16)],
        compiler_params=_params("arbitrary"),
        name="layer0_tail",
    )(o, sg, x, w_out, mix_post, *ffn_w, s5_pre, s5_w_in)


def _layer1_tail_kernel(y_ref, x_ref, wglu_ref, wo_ref, mpost_ref,
                        fpre_ref, wg_ref, wu_ref, wd_ref, fpost_ref, out_ref):
    ms = []
    for r in SUB_TILES:
        y = y_ref[r, :]
        gated = (y.astype(F32) * _sigmoid(_dot(y, wglu_ref[...]))).astype(BF16)
        ms.append(_dot(gated, wo_ref[...]))
    x3s = [x_ref[r, :] + _rmsnorm(m, mpost_ref[...]) for r, m in zip(SUB_TILES, ms)]
    outs = _ffn_body(x3s, fpre_ref, wg_ref, wu_ref, wd_ref, fpost_ref)
    for r, out in zip(SUB_TILES, outs):
        out_ref[r, :] = out


def _layer1_tail(y, x, w_glu, w_out, mix_post, ffn_w):
    return pl.pallas_call(
        _layer1_tail_kernel,
        grid=(N_TOK // ROW_TILE,),
        in_specs=[_row_spec(), _row_spec(),
                  _const_spec((D_MODEL, D_MODEL)), _const_spec((D_MODEL, D_MODEL)),
                  _const_spec((1, D_MODEL)), *_ffn_specs(1)],
        out_specs=_row_spec(),
        out_shape=jax.ShapeDtypeStruct((N_TOK, D_MODEL), F32),
        compiler_params=_params("arbitrary"),
        name="layer1_tail",
    )(y, x, w_glu, w_out, mix_post, *ffn_w)


def _s5_disc_kernel(ldt_ref, lre_ref, lim_ref, bre_ref, bim_ref, cre_ref, cim_ref,
                    are_ref, aim_ref, bdbre_ref, bdbim_ref, bdcre_ref, bdcim_ref):
    dt = jnp.exp(ldt_ref[...])
    lam_re = lre_ref[...]
    lam_im = lim_ref[...]
    mag = jnp.exp(lam_re * dt)
    a_re = mag * jnp.cos(lam_im * dt)
    a_im = mag * jnp.sin(lam_im * dt)
    den = lam_re * lam_re + lam_im * lam_im
    n_re = a_re - 1.0
    z_re = (n_re * lam_re + a_im * lam_im) / den
    z_im = (a_im * lam_re - n_re * lam_im) / den
    b_re = bre_ref[...]
    b_im = bim_ref[...]
    planes = (
        (bdbre_ref, z_re * b_re - z_im * b_im),
        (bdbim_ref, z_re * b_im + z_im * b_re),
        (bdcre_ref, cre_ref[...]),
        (bdcim_ref, cim_ref[...]),
    )
    for dst, _ in planes:
        dst[...] = jnp.zeros_like(dst)
    zeros = jnp.zeros((SSM_GROUP, STATE), F32)
    for blk in range(N_GROUP_BLOCKS):
        for m in range(GROUP_BLOCK // 2):
            g0 = blk * GROUP_BLOCK + 2 * m
            rows = slice(2 * m * SSM_GROUP, 2 * (m + 1) * SSM_GROUP)
            cols = slice(2 * m * STATE, 2 * (m + 1) * STATE)
            for dst, src in planes:
                top = jnp.concatenate([src[g0], zeros], axis=1)
                bot = jnp.concatenate([zeros, src[g0 + 1]], axis=1)
                dst[blk, rows, cols] = jnp.concatenate([top, bot], axis=0).astype(BF16)
            are_ref[blk, :, cols] = jnp.concatenate([a_re[g0], a_re[g0 + 1]], axis=1)
            aim_ref[blk, :, cols] = jnp.concatenate([a_im[g0], a_im[g0 + 1]], axis=1)


def _s5_discretise(log_dt, lam_re, lam_im, b_re_t, b_im_t, c_re, c_im):
    a_shape = jax.ShapeDtypeStruct((N_GROUP_BLOCKS, 1, GB_ST), F32)
    bd_shape = jax.ShapeDtypeStruct((N_GROUP_BLOCKS, GB_CH, GB_ST), BF16)
    return pl.pallas_call(
        _s5_disc_kernel,
        out_shape=[a_shape, a_shape, bd_shape, bd_shape, bd_shape, bd_shape],
        name="s5_discretise",
    )(log_dt, lam_re, lam_im, b_re_t, b_im_t, c_re, c_im)


def _s5_scan_kernel(u_ref, bre_ref, bim_ref, cre_ref, cim_ref, are_ref, aim_ref, d_ref,
                    y_ref, p_ref, pt_ref, xr_ref, xi_ref, sr_ref, si_ref):
    @pl.when(pl.program_id(0) == 0)
    def _():
        sr_ref[...] = jnp.zeros_like(sr_ref)
        si_ref[...] = jnp.zeros_like(si_ref)
        r = lax.broadcasted_iota(jnp.int32, (PERM_ROWS, PERM_ROWS), 0)
        c = lax.broadcasted_iota(jnp.int32, (PERM_ROWS, PERM_ROWS), 1)
        bits = BATCH.bit_length() - 1
        p_ref[...] = (c == (r & (BATCH - 1)) * PERM_STEPS + (r >> bits)).astype(BF16)
        pt_ref[...] = (r == (c & (BATCH - 1)) * PERM_STEPS + (c >> bits)).astype(BF16)

    u_f32 = jnp.concatenate(
        [_dot(p_ref[...], u_ref[:, t0:t0 + PERM_STEPS, :].reshape(PERM_ROWS, D_MODEL))
         for t0 in range(0, SCAN_STEPS, PERM_STEPS)], axis=0)
    u = u_f32.astype(BF16)

    def channels(g):
        return slice(g * GB_CH, (g + 1) * GB_CH)

    def drive(g):
        xr_ref[g] = _dot(u[:, channels(g)], bre_ref[g])
        xi_ref[g] = _dot(u[:, channels(g)], bim_ref[g])

    def recur(g):
        a_re = jnp.broadcast_to(are_ref[g], (BATCH, GB_ST))
        a_im = jnp.broadcast_to(aim_ref[g], (BATCH, GB_ST))
        s_re = sr_ref[g]
        s_im = si_ref[g]
        for t in range(SCAN_STEPS):
            rows = slice(t * BATCH, (t + 1) * BATCH)
            n_re = a_re * s_re - a_im * s_im + xr_ref[g, rows, :]
            n_im = a_re * s_im + a_im * s_re + xi_ref[g, rows, :]
            xr_ref[g, rows, :] = n_re
            xi_ref[g, rows, :] = n_im
            s_re, s_im = n_re, n_im
        sr_ref[g] = s_re
        si_ref[g] = s_im

    def readout(g):
        ch = channels(g)
        y = (_dot_nt(xr_ref[g].astype(BF16), cre_ref[g])
             - _dot_nt(xi_ref[g].astype(BF16), cim_ref[g]))
        y = y + d_ref[:, ch] * u_f32[:, ch]
        y = 0.5 * y * (1.0 + jnp.tanh(math.sqrt(2.0 / math.pi) * (y + 0.044715 * (y * y * y))))
        y = y.astype(BF16)
        for t0 in range(0, SCAN_STEPS, PERM_STEPS):
            y_bm = _dot(pt_ref[...], y[t0 * BATCH:(t0 + PERM_STEPS) * BATCH]).astype(BF16)
            y_ref[:, t0:t0 + PERM_STEPS, ch] = y_bm.reshape(BATCH, PERM_STEPS, GB_CH)

    drive(0)
    for g in range(N_GROUP_BLOCKS):
        if g + 1 < N_GROUP_BLOCKS:
            drive(g + 1)
        recur(g)
        readout(g)


def _s5_scan(u, b_re, b_im, c_re, c_im, a_re, a_im, d_skip):
    tile = pl.BlockSpec((BATCH, SCAN_STEPS, D_MODEL), lambda t: (0, t, 0))
    bd_shape = (N_GROUP_BLOCKS, GB_CH, GB_ST)
    a_shape = (N_GROUP_BLOCKS, 1, GB_ST)
    return pl.pallas_call(
        _s5_scan_kernel,
        grid=(SEQ // SCAN_STEPS,),
        in_specs=[tile,
                  _const_spec(bd_shape), _const_spec(bd_shape),
                  _const_spec(bd_shape), _const_spec(bd_shape),
                  _const_spec(a_shape), _const_spec(a_shape),
                  _const_spec((1, D_MODEL))],
        out_specs=tile,
        out_shape=jax.ShapeDtypeStruct((BATCH, SEQ, D_MODEL), BF16),
        scratch_shapes=[pltpu.VMEM((PERM_ROWS, PERM_ROWS), BF16),
                        pltpu.VMEM((PERM_ROWS, PERM_ROWS), BF16),
                        pltpu.VMEM((N_GROUP_BLOCKS, SCAN_ROWS, GB_ST), F32),
                        pltpu.VMEM((N_GROUP_BLOCKS, SCAN_ROWS, GB_ST), F32),
                        pltpu.VMEM((N_GROUP_BLOCKS, BATCH, GB_ST), F32),
                        pltpu.VMEM((N_GROUP_BLOCKS, BATCH, GB_ST), F32)],
        compiler_params=_params("arbitrary"),
        name="s5_scan",
    )(u.reshape(BATCH, SEQ, D_MODEL), b_re, b_im, c_re, c_im, a_re, a_im, d_skip)


def kernel(x, fox_w_in, fox_b_f, fox_q_gain, fox_k_gain, fox_w_out, s5_w_in, s5_log_dt, s5_lam_re, s5_lam_im, s5_b_re, s5_b_im, s5_c_re, s5_c_im, s5_d, s5_w_glu, s5_w_out, mix_pre_gain, mix_post_gain, ffn_pre_gain, ffn_post_gain, ffn_w_gate, ffn_w_up, ffn_w_down):
    row = lambda a: a.reshape(1, -1).astype(F32)
    x0 = x.reshape(N_TOK, D_MODEL)

    w_gate, w_up, w_down = (w.astype(BF16) for w in (ffn_w_gate, ffn_w_up, ffn_w_down))

    def ffn_weights(i):
        return (row(ffn_pre_gain[i]), w_gate, w_up, w_down, row(ffn_post_gain[i]))

    w_main = fox_w_in.astype(BF16)
    w_f = jnp.pad(fox_w_in[0][:, 4 * D_ATT:], ((0, 0), (0, LANES - N_HEADS))).astype(BF16)
    b_f = jnp.pad(fox_b_f[0], (0, LANES - N_HEADS)).reshape(1, LANES).astype(F32)
    reps = 256 // HEAD_DIM
    q_gain = row(jnp.tile(fox_q_gain[0] * (HEAD_DIM ** -0.5 * LOG2E), reps))
    k_gain = row(jnp.tile(fox_k_gain[0], reps))
    head_of = jnp.arange(256) // HEAD_DIM
    e_blk = (head_of[:, None] == head_of[None, :]).astype(BF16)

    qt, k, vt, sg, c = _fox_in(x0, row(mix_pre_gain[0]), w_main, w_f, b_f, q_gain, k_gain, e_blk)
    o = _attention(qt, k, vt, c)
    x2, u = _layer0_tail(o, sg, x0, fox_w_out[0].astype(BF16), row(mix_post_gain[0]),
                         ffn_weights(0), row(mix_pre_gain[1]), s5_w_in[0].astype(BF16))

    gps = (N_GROUPS, 1, STATE)
    a_re, a_im, bd_b_re, bd_b_im, bd_c_re, bd_c_im = _s5_discretise(
        s5_log_dt[0].reshape(N_GROUPS, 1, 1).astype(F32),
        s5_lam_re[0].reshape(gps).astype(F32), s5_lam_im[0].reshape(gps).astype(F32),
        s5_b_re[0].transpose(0, 2, 1).astype(F32), s5_b_im[0].transpose(0, 2, 1).astype(F32),
        s5_c_re[0].astype(F32), s5_c_im[0].astype(F32))
    y = _s5_scan(u, bd_b_re, bd_b_im, bd_c_re, bd_c_im, a_re, a_im, row(s5_d[0]))
    x4 = _layer1_tail(y.reshape(N_TOK, D_MODEL), x2, s5_w_glu[0].astype(BF16),
                      s5_w_out[0].astype(BF16), row(mix_post_gain[1]), ffn_weights(1))
    return x4.reshape(BATCH, SEQ, D_MODEL)
```

```python
import math

import jax
import jax.numpy as jnp
from jax import lax
from jax.experimental import pallas as pl
from jax.experimental.pallas import tpu as pltpu

F32 = jnp.float32
BF16 = jnp.bfloat16

D_MODEL = 1024
BATCH = 16
SEQ = 2048
N_TOK = BATCH * SEQ
N_HEADS = 16
HEAD_DIM = 64
D_ATT = N_HEADS * HEAD_DIM
SSM_GROUP = 16
N_GROUPS = 64
STATE = 64
D_FF = 2816
EPS = 1e-6

LANES = 128
ROW_TILE = 512
TILES_PER_SEQ = SEQ // ROW_TILE
SUB_TILES = (slice(0, ROW_TILE // 2), slice(ROW_TILE // 2, ROW_TILE))
ATT_TQ = 256
ATT_LOOKAHEAD = 2
ATT_ONES_ROWS = 16
LOG2E = math.log2(math.e)
HEAD_PAIRS = N_HEADS // 2
SCAN_STEPS = 32
SCAN_ROWS = SCAN_STEPS * BATCH
PERM_STEPS = 16
PERM_ROWS = PERM_STEPS * BATCH
GROUP_BLOCK = 16
N_GROUP_BLOCKS = N_GROUPS // GROUP_BLOCK
GB_CH = GROUP_BLOCK * SSM_GROUP
GB_ST = GROUP_BLOCK * STATE
MXU_DIM = 256
_FF_SPLIT = (D_FF // MXU_DIM // 2) * MXU_DIM
FF_CHUNKS = ((0, _FF_SPLIT), (_FF_SPLIT, D_FF))
VMEM_LIMIT = 56 * 1024 * 1024


def _const_spec(shape):
    nd = len(shape)
    return pl.BlockSpec(shape, lambda *_: (0,) * nd, pipeline_mode=pl.Buffered(1))


def _rmsnorm(xf, gain):
    return xf * lax.rsqrt(jnp.mean(xf * xf, axis=-1, keepdims=True) + EPS) * gain


def _dot(a, b):
    return jnp.dot(a, b, preferred_element_type=F32)


def _dot_nt(a, b):
    return lax.dot_general(a, b, (((1,), (1,)), ((), ())), preferred_element_type=F32)


def _sigmoid(x):
    return 1.0 / (1.0 + jnp.exp(-x))


def _params(*sem):
    return pltpu.CompilerParams(dimension_semantics=sem, vmem_limit_bytes=VMEM_LIMIT)


def _fox_in_kernel(x_ref, g_ref, w_ref, wf_ref, bf_ref, qg_ref, kg_ref,
                   qt_ref, k_ref, vt_ref, sg_ref, c_ref, carry_ref):
    i = pl.program_id(0)
    hs = [_rmsnorm(x_ref[r, :], g_ref[...]).astype(BF16) for r in SUB_TILES]

    z = jnp.concatenate([_dot(h, wf_ref[...]) for h in hs], axis=0) + bf_ref[...]
    log_f = jnp.minimum(z, 0.0) - jnp.log(1.0 + jnp.exp(-jnp.abs(z)))
    rows = lax.broadcasted_iota(jnp.int32, (ROW_TILE, ROW_TILE), 0)
    cols = lax.broadcasted_iota(jnp.int32, (ROW_TILE, ROW_TILE), 1)
    tril = (rows >= cols).astype(BF16)
    hi = log_f.astype(BF16).astype(F32)
    r1 = log_f - hi
    mid = r1.astype(BF16).astype(F32)
    lo3 = r1 - mid
    lane = lax.broadcasted_iota(jnp.int32, (ROW_TILE, LANES), 1)
    packed = jnp.where(lane < N_HEADS, hi,
                       jnp.where(lane < 2 * N_HEADS, pltpu.roll(mid, N_HEADS, 1),
                                 pltpu.roll(lo3, 2 * N_HEADS, 1)))
    sums = _dot(tril, packed.astype(BF16))
    cum = (sums + pltpu.roll(sums, LANES - N_HEADS, 1)
           + pltpu.roll(sums, LANES - 2 * N_HEADS, 1))

    def head_norm_t(raw_t, gain_ref):
        out = []
        for hd in range(0, D_ATT, HEAD_DIM):
            rows = raw_t[hd:hd + HEAD_DIM]
            ssq = jnp.sum(rows * rows, axis=0, keepdims=True)
            out.append(rows * lax.rsqrt(ssq * (1.0 / HEAD_DIM) + EPS) * gain_ref[...])
        return jnp.concatenate(out, axis=0)

    q_raws = [_dot(h, w_ref[:, :D_ATT]) for h in hs]
    k_raws = [_dot(h, w_ref[:, D_ATT:2 * D_ATT]) for h in hs]
    for r, q_raw in zip(SUB_TILES, q_raws):
        qt_ref[:, r] = head_norm_t(q_raw.T, qg_ref).astype(BF16)
    for r, h in zip(SUB_TILES, hs):
        vt_ref[:, r] = _dot(h, w_ref[:, 2 * D_ATT:3 * D_ATT]).T.astype(BF16)
    for r, k_raw in zip(SUB_TILES, k_raws):
        k_ref[r, :] = head_norm_t(k_raw.T, kg_ref).T.astype(BF16)
    for r, h in zip(SUB_TILES, hs):
        sg_ref[r, :] = _sigmoid(_dot(h, w_ref[:, 3 * D_ATT:4 * D_ATT])).astype(BF16)

    @pl.when(i % TILES_PER_SEQ == 0)
    def _():
        carry_ref[...] = jnp.zeros_like(carry_ref)

    cum = cum + carry_ref[...]
    c_ref[...] = cum * LOG2E
    carry_ref[...] = cum[ROW_TILE - 1:ROW_TILE, :]


def _fox_in(x, pre_gain, w_main, w_f, b_f, q_gain, k_gain):
    row_spec = pl.BlockSpec((ROW_TILE, D_MODEL), lambda i: (i, 0))
    col_spec = pl.BlockSpec((D_ATT, ROW_TILE), lambda i: (0, i))
    act = jax.ShapeDtypeStruct((N_TOK, D_ATT), BF16)
    act_t = jax.ShapeDtypeStruct((D_ATT, N_TOK), BF16)
    return pl.pallas_call(
        _fox_in_kernel,
        grid=(N_TOK // ROW_TILE,),
        in_specs=[
            row_spec,
            _const_spec((1, D_MODEL)),
            _layer_spec((D_MODEL, 4 * D_ATT), 0),
            _const_spec((D_MODEL, LANES)),
            _const_spec((1, LANES)),
            _const_spec((HEAD_DIM, 1)),
            _const_spec((HEAD_DIM, 1)),
        ],
        out_specs=[col_spec, row_spec, col_spec, row_spec,
                   pl.BlockSpec((ROW_TILE, LANES), lambda i: (i, 0))],
        out_shape=[act_t, act, act_t, act, jax.ShapeDtypeStruct((N_TOK, LANES), F32)],
        scratch_shapes=[pltpu.VMEM((1, LANES), F32)],
        compiler_params=_params("arbitrary"),
        name="fox_in",
    )(x, pre_gain, w_main, w_f, b_f, q_gain, k_gain)


def _attn_kernel(qt_ref, k_ref, vt_ref, c_ref, o_ref):
    pair = pl.program_id(1)
    k = k_ref[...]
    qt = qt_ref[...].astype(F32)
    low_rows = lax.broadcasted_iota(jnp.int32, (LANES, SEQ), 0) < HEAD_DIM
    qmt = (jnp.where(low_rows, qt, 0.0).astype(BF16), jnp.where(low_rows, 0.0, qt).astype(BF16))
    ones = jnp.ones((ATT_ONES_ROWS, SEQ), BF16)
    vmt = tuple(jnp.concatenate([vt_ref[hh * HEAD_DIM:(hh + 1) * HEAD_DIM, :], ones], axis=0)
                for hh in range(2))
    c_all = c_ref[...]
    lane = lax.broadcasted_iota(jnp.int32, (SEQ, LANES), 1)
    cb = tuple(
        jnp.broadcast_to(
            jnp.sum(jnp.where(lane == 2 * pair + hh, c_all, 0.0), axis=-1, keepdims=True),
            (SEQ, ATT_TQ))
        for hh in range(2))
    key = lax.broadcasted_iota(jnp.int32, (ATT_TQ, ATT_TQ), 0)
    qry = lax.broadcasted_iota(jnp.int32, (ATT_TQ, ATT_TQ), 1)
    causal = key <= qry

    def scores(qi, hh):
        k0 = qi * ATT_TQ
        qcols = qmt[hh][:, k0:k0 + ATT_TQ]
        s_d = _dot(k[k0:k0 + ATT_TQ], qcols)
        s_o = _dot(k[:k0], qcols) if qi else None
        return s_d, s_o

    def attend(qi, hh, s_d, s_o):
        k0 = qi * ATT_TQ
        s_d = jnp.where(causal, s_d - cb[hh][k0:k0 + ATT_TQ], -1e30)
        m = jnp.max(s_d, axis=0, keepdims=True)
        if qi:
            s_o = s_o - cb[hh][:k0]
            m = jnp.maximum(m, jnp.max(s_o, axis=0, keepdims=True))
        acc = _dot(vmt[hh][:, k0:k0 + ATT_TQ], jnp.exp2(s_d - m).astype(BF16))
        if qi:
            acc = acc + _dot(vmt[hh][:, :k0], jnp.exp2(s_o - m).astype(BF16))
        return acc[:HEAD_DIM] / acc[HEAD_DIM:HEAD_DIM + 1]

    items = [(qi, hh) for qi in reversed(range(SEQ // ATT_TQ)) for hh in range(2)]
    ahead = [scores(*it) for it in items[:ATT_LOOKAHEAD]]
    outs = []
    for n, (qi, hh) in enumerate(items):
        current = ahead.pop(0)
        if n + ATT_LOOKAHEAD < len(items):
            ahead.append(scores(*items[n + ATT_LOOKAHEAD]))
        outs.append(attend(qi, hh, *current))
        if hh == 1:
            k0 = qi * ATT_TQ
            o_ref[k0:k0 + ATT_TQ, :] = jnp.concatenate(outs, axis=0).T.astype(BF16)
            outs = []


def _attention(qt, k, vt, c):
    seq_spec = pl.BlockSpec((SEQ, LANES), lambda b, j: (b, j))
    seq_t_spec = pl.BlockSpec((LANES, SEQ), lambda b, j: (j, b))
    return pl.pallas_call(
        _attn_kernel,
        grid=(BATCH, HEAD_PAIRS),
        in_specs=[seq_t_spec, seq_spec, seq_t_spec,
                  pl.BlockSpec((SEQ, LANES), lambda b, j: (b, 0))],
        out_specs=seq_spec,
        out_shape=jax.ShapeDtypeStruct((N_TOK, D_ATT), BF16),
        compiler_params=_params("arbitrary", "arbitrary"),
        name="fox_attention",
    )(qt, k, vt, c)


def _ffn_body(xs, pre_ref, wg_ref, wu_ref, wd_ref, post_ref):
    hs = [_rmsnorm(x, pre_ref[...]).astype(BF16) for x in xs]
    accs = [None] * len(xs)
    for c0, c1 in FF_CHUNKS:
        for n, h in enumerate(hs):
            g = _dot(h, wg_ref[:, c0:c1])
            u = _dot(h, wu_ref[:, c0:c1])
            a = (g * _sigmoid(g) * u).astype(BF16)
            part = _dot(a, wd_ref[c0:c1, :])
            accs[n] = part if accs[n] is None else accs[n] + part
    return [x + _rmsnorm(acc, post_ref[...]) for x, acc in zip(xs, accs)]


def _row_spec():
    return pl.BlockSpec((ROW_TILE, D_MODEL), lambda i: (i, 0))


def _layer_spec(shape, layer):
    nd = len(shape)
    return pl.BlockSpec((None,) + shape, lambda *_: (layer,) + (0,) * nd,
                        pipeline_mode=pl.Buffered(1))


def _ffn_specs(layer):
    return [_const_spec((1, D_MODEL)), _layer_spec((D_MODEL, D_FF), layer),
            _layer_spec((D_MODEL, D_FF), layer), _layer_spec((D_FF, D_MODEL), layer),
            _const_spec((1, D_MODEL))]


def _layer0_tail_kernel(o_ref, sg_ref, x_ref, wo_ref, mpost_ref,
                        fpre_ref, wg_ref, wu_ref, wd_ref, fpost_ref,
                        spre_ref, wsin_ref, x2_ref, u_ref):
    ms = [_dot(o_ref[r, :] * sg_ref[r, :], wo_ref[...]) for r in SUB_TILES]
    x1s = [x_ref[r, :] + _rmsnorm(m, mpost_ref[...]) for r, m in zip(SUB_TILES, ms)]
    x2s = _ffn_body(x1s, fpre_ref, wg_ref, wu_ref, wd_ref, fpost_ref)
    for r, x2 in zip(SUB_TILES, x2s):
        x2_ref[r, :] = x2
        u_ref[r, :] = _dot(_rmsnorm(x2, spre_ref[...]).astype(BF16), wsin_ref[...]).astype(BF16)


def _layer0_tail(o, sg, x, w_out, mix_post, ffn_w, s5_pre, s5_w_in):
    return pl.pallas_call(
        _layer0_tail_kernel,
        grid=(N_TOK // ROW_TILE,),
        in_specs=[_row_spec(), _row_spec(), _row_spec(),
                  _const_spec((D_ATT, D_MODEL)), _const_spec((1, D_MODEL)),
                  *_ffn_specs(0),
                  _const_spec((1, D_MODEL)), _const_spec((D_MODEL, D_MODEL))],
        out_specs=[_row_spec(), _row_spec()],
        out_shape=[jax.ShapeDtypeStruct((N_TOK, D_MODEL), F32),
                   jax.ShapeDtypeStruct((N_TOK, D_MODEL), BF16)],
        compiler_params=_params("arbitrary"),
        name="layer0_tail",
    )(o, sg, x, w_out, mix_post, *ffn_w, s5_pre, s5_w_in)


def _layer1_tail_kernel(y_ref, x_ref, wglu_ref, wo_ref, mpost_ref,
                        fpre_ref, wg_ref, wu_ref, wd_ref, fpost_ref, out_ref):
    ms = []
    for r in SUB_TILES:
        y = y_ref[r, :]
        gated = (y.astype(F32) * _sigmoid(_dot(y, wglu_ref[...]))).astype(BF16)
        ms.append(_dot(gated, wo_ref[...]))
    x3s = [x_ref[r, :] + _rmsnorm(m, mpost_ref[...]) for r, m in zip(SUB_TILES, ms)]
    outs = _ffn_body(x3s, fpre_ref, wg_ref, wu_ref, wd_ref, fpost_ref)
    for r, out in zip(SUB_TILES, outs):
        out_ref[r, :] = out


def _layer1_tail(y, x, w_glu, w_out, mix_post, ffn_w):
    return pl.pallas_call(
        _layer1_tail_kernel,
        grid=(N_TOK // ROW_TILE,),
        in_specs=[_row_spec(), _row_spec(),
                  _const_spec((D_MODEL, D_MODEL)), _const_spec((D_MODEL, D_MODEL)),
                  _const_spec((1, D_MODEL)), *_ffn_specs(1)],
        out_specs=_row_spec(),
        out_shape=jax.ShapeDtypeStruct((N_TOK, D_MODEL), F32),
        compiler_params=_params("arbitrary"),
        name="layer1_tail",
    )(y, x, w_glu, w_out, mix_post, *ffn_w)


def _s5_disc_kernel(ldt_ref, lre_ref, lim_ref, bre_ref, bim_ref, cre_ref, cim_ref,
                    are_ref, aim_ref, bdbre_ref, bdbim_ref, bdcre_ref, bdcim_ref):
    dt = jnp.exp(ldt_ref[...])
    lam_re = lre_ref[...]
    lam_im = lim_ref[...]
    mag = jnp.exp(lam_re * dt)
    a_re = mag * jnp.cos(lam_im * dt)
    a_im = mag * jnp.sin(lam_im * dt)
    den = lam_re * lam_re + lam_im * lam_im
    n_re = a_re - 1.0
    z_re = (n_re * lam_re + a_im * lam_im) / den
    z_im = (a_im * lam_re - n_re * lam_im) / den
    b_re = bre_ref[...]
    b_im = bim_ref[...]
    planes = (
        (bdbre_ref, z_re * b_re - z_im * b_im),
        (bdbim_ref, z_re * b_im + z_im * b_re),
        (bdcre_ref, cre_ref[...]),
        (bdcim_ref, cim_ref[...]),
    )
    for dst, _ in planes:
        dst[...] = jnp.zeros_like(dst)
    zeros = jnp.zeros((SSM_GROUP, STATE), F32)
    for blk in range(N_GROUP_BLOCKS):
        for m in range(GROUP_BLOCK // 2):
            g0 = blk * GROUP_BLOCK + 2 * m
            rows = slice(2 * m * SSM_GROUP, 2 * (m + 1) * SSM_GROUP)
            cols = slice(2 * m * STATE, 2 * (m + 1) * STATE)
            for dst, src in planes:
                top = jnp.concatenate([src[g0], zeros], axis=1)
                bot = jnp.concatenate([zeros, src[g0 + 1]], axis=1)
                dst[blk, rows, cols] = jnp.concatenate([top, bot], axis=0).astype(BF16)
            are_ref[blk, :, cols] = jnp.concatenate([a_re[g0], a_re[g0 + 1]], axis=1)
            aim_ref[blk, :, cols] = jnp.concatenate([a_im[g0], a_im[g0 + 1]], axis=1)


def _s5_discretise(log_dt, lam_re, lam_im, b_re_t, b_im_t, c_re, c_im):
    a_shape = jax.ShapeDtypeStruct((N_GROUP_BLOCKS, 1, GB_ST), F32)
    bd_shape = jax.ShapeDtypeStruct((N_GROUP_BLOCKS, GB_CH, GB_ST), BF16)
    return pl.pallas_call(
        _s5_disc_kernel,
        out_shape=[a_shape, a_shape, bd_shape, bd_shape, bd_shape, bd_shape],
        name="s5_discretise",
    )(log_dt, lam_re, lam_im, b_re_t, b_im_t, c_re, c_im)


def _s5_scan_kernel(u_ref, bre_ref, bim_ref, cre_ref, cim_ref, are_ref, aim_ref, d_ref,
                    y_ref, p_ref, pt_ref, xr_ref, xi_ref, sr_ref, si_ref):
    @pl.when(pl.program_id(0) == 0)
    def _():
        sr_ref[...] = jnp.zeros_like(sr_ref)
        si_ref[...] = jnp.zeros_like(si_ref)
        r = lax.broadcasted_iota(jnp.int32, (PERM_ROWS, PERM_ROWS), 0)
        c = lax.broadcasted_iota(jnp.int32, (PERM_ROWS, PERM_ROWS), 1)
        bits = BATCH.bit_length() - 1
        p_ref[...] = (c == (r & (BATCH - 1)) * PERM_STEPS + (r >> bits)).astype(BF16)
        pt_ref[...] = (r == (c & (BATCH - 1)) * PERM_STEPS + (c >> bits)).astype(BF16)

    u_f32 = jnp.concatenate(
        [_dot(p_ref[...], u_ref[:, t0:t0 + PERM_STEPS, :].reshape(PERM_ROWS, D_MODEL))
         for t0 in range(0, SCAN_STEPS, PERM_STEPS)], axis=0)
    u = u_f32.astype(BF16)

    def channels(g):
        return slice(g * GB_CH, (g + 1) * GB_CH)

    def drive(g):
        xr_ref[g] = _dot(u[:, channels(g)], bre_ref[g])
        xi_ref[g] = _dot(u[:, channels(g)], bim_ref[g])

    def recur(g):
        a_re = jnp.broadcast_to(are_ref[g], (BATCH, GB_ST))
        a_im = jnp.broadcast_to(aim_ref[g], (BATCH, GB_ST))
        s_re = sr_ref[g]
        s_im = si_ref[g]
        for t in range(SCAN_STEPS):
            rows = slice(t * BATCH, (t + 1) * BATCH)
            n_re = a_re * s_re - a_im * s_im + xr_ref[g, rows, :]
            n_im = a_re * s_im + a_im * s_re + xi_ref[g, rows, :]
            xr_ref[g, rows, :] = n_re
            xi_ref[g, rows, :] = n_im
            s_re, s_im = n_re, n_im
        sr_ref[g] = s_re
        si_ref[g] = s_im

    def readout(g):
        ch = channels(g)
        y = (_dot_nt(xr_ref[g].astype(BF16), cre_ref[g])
             - _dot_nt(xi_ref[g].astype(BF16), cim_ref[g]))
        y = y + d_ref[:, ch] * u_f32[:, ch]
        y = 0.5 * y * (1.0 + jnp.tanh(math.sqrt(2.0 / math.pi) * (y + 0.044715 * (y * y * y))))
        y = y.astype(BF16)
        for t0 in range(0, SCAN_STEPS, PERM_STEPS):
            y_bm = _dot(pt_ref[...], y[t0 * BATCH:(t0 + PERM_STEPS) * BATCH]).astype(BF16)
            y_ref[:, t0:t0 + PERM_STEPS, ch] = y_bm.reshape(BATCH, PERM_STEPS, GB_CH)

    drive(0)
    for g in range(N_GROUP_BLOCKS):
        if g + 1 < N_GROUP_BLOCKS:
            drive(g + 1)
        recur(g)
        readout(g)


def _s5_scan(u, b_re, b_im, c_re, c_im, a_re, a_im, d_skip):
    tile = pl.BlockSpec((BATCH, SCAN_STEPS, D_MODEL), lambda t: (0, t, 0))
    bd_shape = (N_GROUP_BLOCKS, GB_CH, GB_ST)
    a_shape = (N_GROUP_BLOCKS, 1, GB_ST)
    return pl.pallas_call(
        _s5_scan_kernel,
        grid=(SEQ // SCAN_STEPS,),
        in_specs=[tile,
                  _const_spec(bd_shape), _const_spec(bd_shape),
                  _const_spec(bd_shape), _const_spec(bd_shape),
                  _const_spec(a_shape), _const_spec(a_shape),
                  _const_spec((1, D_MODEL))],
        out_specs=tile,
        out_shape=jax.ShapeDtypeStruct((BATCH, SEQ, D_MODEL), BF16),
        scratch_shapes=[pltpu.VMEM((PERM_ROWS, PERM_ROWS), BF16),
                        pltpu.VMEM((PERM_ROWS, PERM_ROWS), BF16),
                        pltpu.VMEM((N_GROUP_BLOCKS, SCAN_ROWS, GB_ST), F32),
                        pltpu.VMEM((N_GROUP_BLOCKS, SCAN_ROWS, GB_ST), F32),
                        pltpu.VMEM((N_GROUP_BLOCKS, BATCH, GB_ST), F32),
                        pltpu.VMEM((N_GROUP_BLOCKS, BATCH, GB_ST), F32)],
        compiler_params=_params("arbitrary"),
        name="s5_scan",
    )(u.reshape(BATCH, SEQ, D_MODEL), b_re, b_im, c_re, c_im, a_re, a_im, d_skip)


def kernel(x, fox_w_in, fox_b_f, fox_q_gain, fox_k_gain, fox_w_out, s5_w_in, s5_log_dt, s5_lam_re, s5_lam_im, s5_b_re, s5_b_im, s5_c_re, s5_c_im, s5_d, s5_w_glu, s5_w_out, mix_pre_gain, mix_post_gain, ffn_pre_gain, ffn_post_gain, ffn_w_gate, ffn_w_up, ffn_w_down):
    row = lambda a: a.reshape(1, -1).astype(F32)
    x0 = x.reshape(N_TOK, D_MODEL)

    w_gate, w_up, w_down = (w.astype(BF16) for w in (ffn_w_gate, ffn_w_up, ffn_w_down))

    def ffn_weights(i):
        return (row(ffn_pre_gain[i]), w_gate, w_up, w_down, row(ffn_post_gain[i]))

    w_main = fox_w_in.astype(BF16)
    w_f = jnp.pad(fox_w_in[0][:, 4 * D_ATT:], ((0, 0), (0, LANES - N_HEADS))).astype(BF16)
    b_f = jnp.pad(fox_b_f[0], (0, LANES - N_HEADS)).reshape(1, LANES).astype(F32)
    q_gain = (fox_q_gain[0] * (HEAD_DIM ** -0.5 * LOG2E)).reshape(HEAD_DIM, 1).astype(F32)
    k_gain = fox_k_gain[0].reshape(HEAD_DIM, 1).astype(F32)

    qt, k, vt, sg, c = _fox_in(x0, row(mix_pre_gain[0]), w_main, w_f, b_f, q_gain, k_gain)
    o = _attention(qt, k, vt, c)
    x2, u = _layer0_tail(o, sg, x0, fox_w_out[0].astype(BF16), row(mix_post_gain[0]),
                         ffn_weights(0), row(mix_pre_gain[1]), s5_w_in[0].astype(BF16))

    gps = (N_GROUPS, 1, STATE)
    a_re, a_im, bd_b_re, bd_b_im, bd_c_re, bd_c_im = _s5_discretise(
        s5_log_dt[0].reshape(N_GROUPS, 1, 1).astype(F32),
        s5_lam_re[0].reshape(gps).astype(F32), s5_lam_im[0].reshape(gps).astype(F32),
        s5_b_re[0].transpose(0, 2, 1).astype(F32), s5_b_im[0].transpose(0, 2, 1).astype(F32),
        s5_c_re[0].astype(F32), s5_c_im[0].astype(F32))
    y = _s5_scan(u, bd_b_re, bd_b_im, bd_c_re, bd_c_im, a_re, a_im, row(s5_d[0]))
    x4 = _layer1_tail(y.reshape(N_TOK, D_MODEL), x2, s5_w_glu[0].astype(BF16),
                      s5_w_out[0].astype(BF16), row(mix_post_gain[1]), ffn_weights(1))
    return x4.reshape(BATCH, SEQ, D_MODEL)
```

```python
import math

import jax
import jax.numpy as jnp
from jax import lax
from jax.experimental import pallas as pl
from jax.experimental.pallas import tpu as pltpu

F32 = jnp.float32
BF16 = jnp.bfloat16

D_MODEL = 1024
BATCH = 16
SEQ = 2048
N_TOK = BATCH * SEQ
N_HEADS = 16
HEAD_DIM = 64
D_ATT = N_HEADS * HEAD_DIM
SSM_GROUP = 16
N_GROUPS = 64
STATE = 64
D_FF = 2816
EPS = 1e-6

LANES = 128
ROW_TILE = 512
TILES_PER_SEQ = SEQ // ROW_TILE
SUB_TILES = (slice(0, ROW_TILE // 2), slice(ROW_TILE // 2, ROW_TILE))
ATT_TQ = 256
ATT_LOOKAHEAD = 2
ATT_ONES_ROWS = 16
LOG2E = math.log2(math.e)
HEAD_PAIRS = N_HEADS // 2
SCAN_STEPS = 32
SCAN_ROWS = SCAN_STEPS * BATCH
PERM_STEPS = 16
PERM_ROWS = PERM_STEPS * BATCH
GROUP_BLOCK = 16
N_GROUP_BLOCKS = N_GROUPS // GROUP_BLOCK
GB_CH = GROUP_BLOCK * SSM_GROUP
GB_ST = GROUP_BLOCK * STATE
MXU_DIM = 256
_FF_SPLIT = (D_FF // MXU_DIM // 2) * MXU_DIM
FF_CHUNKS = ((0, _FF_SPLIT), (_FF_SPLIT, D_FF))
VMEM_LIMIT = 56 * 1024 * 1024


def _const_spec(shape):
    nd = len(shape)
    return pl.BlockSpec(shape, lambda *_: (0,) * nd, pipeline_mode=pl.Buffered(1))


def _rmsnorm(xf, gain):
    return xf * lax.rsqrt(jnp.mean(xf * xf, axis=-1, keepdims=True) + EPS) * gain


def _dot(a, b):
    return jnp.dot(a, b, preferred_element_type=F32)


def _sigmoid(x):
    return 1.0 / (1.0 + jnp.exp(-x))


def _params(*sem):
    return pltpu.CompilerParams(dimension_semantics=sem, vmem_limit_bytes=VMEM_LIMIT)


def _fox_in_kernel(x_ref, g_ref, w_ref, wf_ref, bf_ref, qg_ref, kg_ref,
                   qt_ref, k_ref, vt_ref, sg_ref, c_ref, carry_ref):
    i = pl.program_id(0)
    hs = [_rmsnorm(x_ref[r, :], g_ref[...]).astype(BF16) for r in SUB_TILES]

    z = jnp.concatenate([_dot(h, wf_ref[...]) for h in hs], axis=0) + bf_ref[...]
    log_f = jnp.minimum(z, 0.0) - jnp.log(1.0 + jnp.exp(-jnp.abs(z)))
    rows = lax.broadcasted_iota(jnp.int32, (ROW_TILE, ROW_TILE), 0)
    cols = lax.broadcasted_iota(jnp.int32, (ROW_TILE, ROW_TILE), 1)
    tril = (rows >= cols).astype(BF16)
    hi = log_f.astype(BF16).astype(F32)
    r1 = log_f - hi
    mid = r1.astype(BF16).astype(F32)
    lo3 = r1 - mid
    lane = lax.broadcasted_iota(jnp.int32, (ROW_TILE, LANES), 1)
    packed = jnp.where(lane < N_HEADS, hi,
                       jnp.where(lane < 2 * N_HEADS, pltpu.roll(mid, N_HEADS, 1),
                                 pltpu.roll(lo3, 2 * N_HEADS, 1)))
    sums = _dot(tril, packed.astype(BF16))
    cum = (sums + pltpu.roll(sums, LANES - N_HEADS, 1)
           + pltpu.roll(sums, LANES - 2 * N_HEADS, 1))

    def head_norm_t(raw_t, gain_ref):
        out = []
        for hd in range(0, D_ATT, HEAD_DIM):
            rows = raw_t[hd:hd + HEAD_DIM]
            ssq = jnp.sum(rows * rows, axis=0, keepdims=True)
            out.append(rows * lax.rsqrt(ssq * (1.0 / HEAD_DIM) + EPS) * gain_ref[...])
        return jnp.concatenate(out, axis=0)

    q_raws = [_dot(h, w_ref[:, :D_ATT]) for h in hs]
    k_raws = [_dot(h, w_ref[:, D_ATT:2 * D_ATT]) for h in hs]
    for r, q_raw in zip(SUB_TILES, q_raws):
        qt_ref[:, r] = head_norm_t(q_raw.T, qg_ref).astype(BF16)
    for r, h in zip(SUB_TILES, hs):
        vt_ref[:, r] = _dot(h, w_ref[:, 2 * D_ATT:3 * D_ATT]).T.astype(BF16)
    for r, k_raw in zip(SUB_TILES, k_raws):
        k_ref[r, :] = head_norm_t(k_raw.T, kg_ref).T.astype(BF16)
    for r, h in zip(SUB_TILES, hs):
        sg_ref[r, :] = _sigmoid(_dot(h, w_ref[:, 3 * D_ATT:4 * D_ATT])).astype(BF16)

    @pl.when(i % TILES_PER_SEQ == 0)
    def _():
        carry_ref[...] = jnp.zeros_like(carry_ref)

    cum = cum + carry_ref[...]
    c_ref[...] = cum * LOG2E
    carry_ref[...] = cum[ROW_TILE - 1:ROW_TILE, :]


def _fox_in(x, pre_gain, w_main, w_f, b_f, q_gain, k_gain):
    row_spec = pl.BlockSpec((ROW_TILE, D_MODEL), lambda i: (i, 0))
    col_spec = pl.BlockSpec((D_ATT, ROW_TILE), lambda i: (0, i))
    act = jax.ShapeDtypeStruct((N_TOK, D_ATT), BF16)
    act_t = jax.ShapeDtypeStruct((D_ATT, N_TOK), BF16)
    return pl.pallas_call(
        _fox_in_kernel,
        grid=(N_TOK // ROW_TILE,),
        in_specs=[
            row_spec,
            _const_spec((1, D_MODEL)),
            _layer_spec((D_MODEL, 4 * D_ATT), 0),
            _const_spec((D_MODEL, LANES)),
            _const_spec((1, LANES)),
            _const_spec((HEAD_DIM, 1)),
            _const_spec((HEAD_DIM, 1)),
        ],
        out_specs=[col_spec, row_spec, col_spec, row_spec,
                   pl.BlockSpec((ROW_TILE, LANES), lambda i: (i, 0))],
        out_shape=[act_t, act, act_t, act, jax.ShapeDtypeStruct((N_TOK, LANES), F32)],
        scratch_shapes=[pltpu.VMEM((1, LANES), F32)],
        compiler_params=_params("arbitrary"),
        name="fox_in",
    )(x, pre_gain, w_main, w_f, b_f, q_gain, k_gain)


def _attn_kernel(qt_ref, k_ref, vt_ref, c_ref, o_ref):
    pair = pl.program_id(1)
    k = k_ref[...]
    qt = qt_ref[...].astype(F32)
    low_rows = lax.broadcasted_iota(jnp.int32, (LANES, SEQ), 0) < HEAD_DIM
    qmt = (jnp.where(low_rows, qt, 0.0).astype(BF16), jnp.where(low_rows, 0.0, qt).astype(BF16))
    ones = jnp.ones((ATT_ONES_ROWS, SEQ), BF16)
    vmt = tuple(jnp.concatenate([vt_ref[hh * HEAD_DIM:(hh + 1) * HEAD_DIM, :], ones], axis=0)
                for hh in range(2))
    c_all = c_ref[...]
    lane = lax.broadcasted_iota(jnp.int32, (SEQ, LANES), 1)
    cb = tuple(
        jnp.broadcast_to(
            jnp.sum(jnp.where(lane == 2 * pair + hh, c_all, 0.0), axis=-1, keepdims=True),
            (SEQ, ATT_TQ))
        for hh in range(2))
    key = lax.broadcasted_iota(jnp.int32, (ATT_TQ, ATT_TQ), 0)
    qry = lax.broadcasted_iota(jnp.int32, (ATT_TQ, ATT_TQ), 1)
    causal = key <= qry

    def scores(qi, hh):
        k0 = qi * ATT_TQ
        qcols = qmt[hh][:, k0:k0 + ATT_TQ]
        s_d = _dot(k[k0:k0 + ATT_TQ], qcols)
        s_o = _dot(k[:k0], qcols) if qi else None
        return s_d, s_o

    def attend(qi, hh, s_d, s_o):
        k0 = qi * ATT_TQ
        s_d = jnp.where(causal, s_d - cb[hh][k0:k0 + ATT_TQ], -1e30)
        m = jnp.max(s_d, axis=0, keepdims=True)
        if qi:
            s_o = s_o - cb[hh][:k0]
            m = jnp.maximum(m, jnp.max(s_o, axis=0, keepdims=True))
        acc = _dot(vmt[hh][:, k0:k0 + ATT_TQ], jnp.exp2(s_d - m).astype(BF16))
        if qi:
            acc = acc + _dot(vmt[hh][:, :k0], jnp.exp2(s_o - m).astype(BF16))
        return acc[:HEAD_DIM] / acc[HEAD_DIM:HEAD_DIM + 1]

    items = [(qi, hh) for qi in range(SEQ // ATT_TQ) for hh in range(2)]
    ahead = [scores(*it) for it in items[:ATT_LOOKAHEAD]]
    outs = []
    for n, (qi, hh) in enumerate(items):
        current = ahead.pop(0)
        if n + ATT_LOOKAHEAD < len(items):
            ahead.append(scores(*items[n + ATT_LOOKAHEAD]))
        outs.append(attend(qi, hh, *current))
        if hh == 1:
            k0 = qi * ATT_TQ
            o_ref[k0:k0 + ATT_TQ, :] = jnp.concatenate(outs, axis=0).T.astype(BF16)
            outs = []


def _attention(qt, k, vt, c):
    seq_spec = pl.BlockSpec((SEQ, LANES), lambda b, j: (b, j))
    seq_t_spec = pl.BlockSpec((LANES, SEQ), lambda b, j: (j, b))
    return pl.pallas_call(
        _attn_kernel,
        grid=(BATCH, HEAD_PAIRS),
        in_specs=[seq_t_spec, seq_spec, seq_t_spec,
                  pl.BlockSpec((SEQ, LANES), lambda b, j: (b, 0))],
        out_specs=seq_spec,
        out_shape=jax.ShapeDtypeStruct((N_TOK, D_ATT), BF16),
        compiler_params=_params("arbitrary", "arbitrary"),
        name="fox_attention",
    )(qt, k, vt, c)


def _ffn_body(xs, pre_ref, wg_ref, wu_ref, wd_ref, post_ref):
    hs = [_rmsnorm(x, pre_ref[...]).astype(BF16) for x in xs]
    accs = [None] * len(xs)
    for c0, c1 in FF_CHUNKS:
        for n, h in enumerate(hs):
            g = _dot(h, wg_ref[:, c0:c1])
            u = _dot(h, wu_ref[:, c0:c1])
            a = (g * _sigmoid(g) * u).astype(BF16)
            part = _dot(a, wd_ref[c0:c1, :])
            accs[n] = part if accs[n] is None else accs[n] + part
    return [x + _rmsnorm(acc, post_ref[...]) for x, acc in zip(xs, accs)]


def _row_spec():
    return pl.BlockSpec((ROW_TILE, D_MODEL), lambda i: (i, 0))


def _layer_spec(shape, layer):
    nd = len(shape)
    return pl.BlockSpec((None,) + shape, lambda *_: (layer,) + (0,) * nd,
                        pipeline_mode=pl.Buffered(1))


def _ffn_specs(layer):
    return [_const_spec((1, D_MODEL)), _layer_spec((D_MODEL, D_FF), layer),
            _layer_spec((D_MODEL, D_FF), layer), _layer_spec((D_FF, D_MODEL), layer),
            _const_spec((1, D_MODEL))]


def _layer0_tail_kernel(o_ref, sg_ref, x_ref, wo_ref, mpost_ref,
                        fpre_ref, wg_ref, wu_ref, wd_ref, fpost_ref,
                        spre_ref, wsin_ref, x2_ref, u_ref):
    ms = [_dot(o_ref[r, :] * sg_ref[r, :], wo_ref[...]) for r in SUB_TILES]
    x1s = [x_ref[r, :] + _rmsnorm(m, mpost_ref[...]) for r, m in zip(SUB_TILES, ms)]
    x2s = _ffn_body(x1s, fpre_ref, wg_ref, wu_ref, wd_ref, fpost_ref)
    for r, x2 in zip(SUB_TILES, x2s):
        x2_ref[r, :] = x2
        u_ref[r, :] = _dot(_rmsnorm(x2, spre_ref[...]).astype(BF16), wsin_ref[...]).astype(BF16)


def _layer0_tail(o, sg, x, w_out, mix_post, ffn_w, s5_pre, s5_w_in):
    return pl.pallas_call(
        _layer0_tail_kernel,
        grid=(N_TOK // ROW_TILE,),
        in_specs=[_row_spec(), _row_spec(), _row_spec(),
                  _const_spec((D_ATT, D_MODEL)), _const_spec((1, D_MODEL)),
                  *_ffn_specs(0),
                  _const_spec((1, D_MODEL)), _const_spec((D_MODEL, D_MODEL))],
        out_specs=[_row_spec(), _row_spec()],
        out_shape=[jax.ShapeDtypeStruct((N_TOK, D_MODEL), F32),
                   jax.ShapeDtypeStruct((N_TOK, D_MODEL), BF16)],
        compiler_params=_params("arbitrary"),
        name="layer0_tail",
    )(o, sg, x, w_out, mix_post, *ffn_w, s5_pre, s5_w_in)


def _layer1_tail_kernel(y_ref, x_ref, wglu_ref, wo_ref, mpost_ref,
                        fpre_ref, wg_ref, wu_ref, wd_ref, fpost_ref, out_ref):
    ms = []
    for r in SUB_TILES:
        y = y_ref[r, :]
        gated = (y.astype(F32) * _sigmoid(_dot(y, wglu_ref[...]))).astype(BF16)
        ms.append(_dot(gated, wo_ref[...]))
    x3s = [x_ref[r, :] + _rmsnorm(m, mpost_ref[...]) for r, m in zip(SUB_TILES, ms)]
    outs = _ffn_body(x3s, fpre_ref, wg_ref, wu_ref, wd_ref, fpost_ref)
    for r, out in zip(SUB_TILES, outs):
        out_ref[r, :] = out


def _layer1_tail(y, x, w_glu, w_out, mix_post, ffn_w):
    return pl.pallas_call(
        _layer1_tail_kernel,
        grid=(N_TOK // ROW_TILE,),
        in_specs=[_row_spec(), _row_spec(),
                  _const_spec((D_MODEL, D_MODEL)), _const_spec((D_MODEL, D_MODEL)),
                  _const_spec((1, D_MODEL)), *_ffn_specs(1)],
        out_specs=_row_spec(),
        out_shape=jax.ShapeDtypeStruct((N_TOK, D_MODEL), F32),
        compiler_params=_params("arbitrary"),
        name="layer1_tail",
    )(y, x, w_glu, w_out, mix_post, *ffn_w)


def _s5_disc_kernel(ldt_ref, lre_ref, lim_ref, bre_ref, bim_ref, cre_ref, cim_ref,
                    are_ref, aim_ref, bdbre_ref, bdbim_ref, bdcre_ref, bdcim_ref,
                    cre_t_ref, cim_t_ref):
    dt = jnp.exp(ldt_ref[...])
    lam_re = lre_ref[...]
    lam_im = lim_ref[...]
    mag = jnp.exp(lam_re * dt)
    a_re = mag * jnp.cos(lam_im * dt)
    a_im = mag * jnp.sin(lam_im * dt)
    den = lam_re * lam_re + lam_im * lam_im
    n_re = a_re - 1.0
    z_re = (n_re * lam_re + a_im * lam_im) / den
    z_im = (a_im * lam_re - n_re * lam_im) / den
    b_re = bre_ref[...]
    b_im = bim_ref[...]
    planes = (
        (bdbre_ref, z_re * b_re - z_im * b_im),
        (bdbim_ref, z_re * b_im + z_im * b_re),
        (cre_t_ref, cre_ref[...]),
        (cim_t_ref, cim_ref[...]),
    )
    for dst, _ in planes:
        dst[...] = jnp.zeros_like(dst)
    zeros = jnp.zeros((SSM_GROUP, STATE), F32)
    for blk in range(N_GROUP_BLOCKS):
        for m in range(GROUP_BLOCK // 2):
            g0 = blk * GROUP_BLOCK + 2 * m
            rows = slice(2 * m * SSM_GROUP, 2 * (m + 1) * SSM_GROUP)
            cols = slice(2 * m * STATE, 2 * (m + 1) * STATE)
            for dst, src in planes:
                top = jnp.concatenate([src[g0], zeros], axis=1)
                bot = jnp.concatenate([zeros, src[g0 + 1]], axis=1)
                dst[blk, rows, cols] = jnp.concatenate([top, bot], axis=0).astype(dst.dtype)
            are_ref[blk, :, cols] = jnp.concatenate([a_re[g0], a_re[g0 + 1]], axis=1)
            aim_ref[blk, :, cols] = jnp.concatenate([a_im[g0], a_im[g0 + 1]], axis=1)
        bdcre_ref[blk] = cre_t_ref[blk].T.astype(BF16)
        bdcim_ref[blk] = cim_t_ref[blk].T.astype(BF16)


def _s5_discretise(log_dt, lam_re, lam_im, b_re_t, b_im_t, c_re, c_im):
    a_shape = jax.ShapeDtypeStruct((N_GROUP_BLOCKS, 1, GB_ST), F32)
    b_shape = jax.ShapeDtypeStruct((N_GROUP_BLOCKS, GB_CH, GB_ST), BF16)
    c_shape = jax.ShapeDtypeStruct((N_GROUP_BLOCKS, GB_ST, GB_CH), BF16)
    return pl.pallas_call(
        _s5_disc_kernel,
        out_shape=[a_shape, a_shape, b_shape, b_shape, c_shape, c_shape],
        scratch_shapes=[pltpu.VMEM((N_GROUP_BLOCKS, GB_CH, GB_ST), F32),
                        pltpu.VMEM((N_GROUP_BLOCKS, GB_CH, GB_ST), F32)],
        compiler_params=_params(),
        name="s5_discretise",
    )(log_dt, lam_re, lam_im, b_re_t, b_im_t, c_re, c_im)


def _s5_scan_kernel(u_ref, bre_ref, bim_ref, cre_ref, cim_ref, are_ref, aim_ref, d_ref,
                    y_ref, p_ref, pt_ref, xr_ref, xi_ref, sr_ref, si_ref):
    @pl.when(pl.program_id(0) == 0)
    def _():
        sr_ref[...] = jnp.zeros_like(sr_ref)
        si_ref[...] = jnp.zeros_like(si_ref)
        r = lax.broadcasted_iota(jnp.int32, (PERM_ROWS, PERM_ROWS), 0)
        c = lax.broadcasted_iota(jnp.int32, (PERM_ROWS, PERM_ROWS), 1)
        bits = BATCH.bit_length() - 1
        p_ref[...] = (c == (r & (BATCH - 1)) * PERM_STEPS + (r >> bits)).astype(BF16)
        pt_ref[...] = (r == (c & (BATCH - 1)) * PERM_STEPS + (c >> bits)).astype(BF16)

    u_f32 = jnp.concatenate(
        [_dot(p_ref[...], u_ref[:, t0:t0 + PERM_STEPS, :].reshape(PERM_ROWS, D_MODEL))
         for t0 in range(0, SCAN_STEPS, PERM_STEPS)], axis=0)
    u = u_f32.astype(BF16)

    def channels(g):
        return slice(g * GB_CH, (g + 1) * GB_CH)

    def drive(g):
        xr_ref[g] = _dot(u[:, channels(g)], bre_ref[g])
        xi_ref[g] = _dot(u[:, channels(g)], bim_ref[g])

    def recur(g):
        a_re = jnp.broadcast_to(are_ref[g], (BATCH, GB_ST))
        a_im = jnp.broadcast_to(aim_ref[g], (BATCH, GB_ST))
        s_re = sr_ref[g]
        s_im = si_ref[g]
        for t in range(SCAN_STEPS):
            rows = slice(t * BATCH, (t + 1) * BATCH)
            n_re = a_re * s_re - a_im * s_im + xr_ref[g, rows, :]
            n_im = a_re * s_im + a_im * s_re + xi_ref[g, rows, :]
            xr_ref[g, rows, :] = n_re
            xi_ref[g, rows, :] = n_im
            s_re, s_im = n_re, n_im
        sr_ref[g] = s_re
        si_ref[g] = s_im

    def readout(g):
        ch = channels(g)
        y = (_dot(xr_ref[g].astype(BF16), cre_ref[g])
             - _dot(xi_ref[g].astype(BF16), cim_ref[g]))
        y = y + d_ref[:, ch] * u_f32[:, ch]
        y = 0.5 * y * (1.0 + jnp.tanh(math.sqrt(2.0 / math.pi) * (y + 0.044715 * (y * y * y))))
        y = y.astype(BF16)
        for t0 in range(0, SCAN_STEPS, PERM_STEPS):
            y_bm = _dot(pt_ref[...], y[t0 * BATCH:(t0 + PERM_STEPS) * BATCH]).astype(BF16)
            y_ref[:, t0:t0 + PERM_STEPS, ch] = y_bm.reshape(BATCH, PERM_STEPS, GB_CH)

    drive(0)
    for g in range(N_GROUP_BLOCKS):
        if g + 1 < N_GROUP_BLOCKS:
            drive(g + 1)
        recur(g)
        readout(g)


def _s5_scan(u, b_re, b_im, c_re, c_im, a_re, a_im, d_skip):
    tile = pl.BlockSpec((BATCH, SCAN_STEPS, D_MODEL), lambda t: (0, t, 0))
    b_shape = (N_GROUP_BLOCKS, GB_CH, GB_ST)
    c_shape = (N_GROUP_BLOCKS, GB_ST, GB_CH)
    a_shape = (N_GROUP_BLOCKS, 1, GB_ST)
    return pl.pallas_call(
        _s5_scan_kernel,
        grid=(SEQ // SCAN_STEPS,),
        in_specs=[tile,
                  _const_spec(b_shape), _const_spec(b_shape),
                  _const_spec(c_shape), _const_spec(c_shape),
                  _const_spec(a_shape), _const_spec(a_shape),
                  _const_spec((1, D_MODEL))],
        out_specs=tile,
        out_shape=jax.ShapeDtypeStruct((BATCH, SEQ, D_MODEL), BF16),
        scratch_shapes=[pltpu.VMEM((PERM_ROWS, PERM_ROWS), BF16),
                        pltpu.VMEM((PERM_ROWS, PERM_ROWS), BF16),
                        pltpu.VMEM((N_GROUP_BLOCKS, SCAN_ROWS, GB_ST), F32),
                        pltpu.VMEM((N_GROUP_BLOCKS, SCAN_ROWS, GB_ST), F32),
                        pltpu.VMEM((N_GROUP_BLOCKS, BATCH, GB_ST), F32),
                        pltpu.VMEM((N_GROUP_BLOCKS, BATCH, GB_ST), F32)],
        compiler_params=_params("arbitrary"),
        name="s5_scan",
    )(u.reshape(BATCH, SEQ, D_MODEL), b_re, b_im, c_re, c_im, a_re, a_im, d_skip)


def kernel(x, fox_w_in, fox_b_f, fox_q_gain, fox_k_gain, fox_w_out, s5_w_in, s5_log_dt, s5_lam_re, s5_lam_im, s5_b_re, s5_b_im, s5_c_re, s5_c_im, s5_d, s5_w_glu, s5_w_out, mix_pre_gain, mix_post_gain, ffn_pre_gain, ffn_post_gain, ffn_w_gate, ffn_w_up, ffn_w_down):
    row = lambda a: a.reshape(1, -1).astype(F32)
    x0 = x.reshape(N_TOK, D_MODEL)

    w_gate, w_up, w_down = (w.astype(BF16) for w in (ffn_w_gate, ffn_w_up, ffn_w_down))

    def ffn_weights(i):
        return (row(ffn_pre_gain[i]), w_gate, w_up, w_down, row(ffn_post_gain[i]))

    w_main = fox_w_in.astype(BF16)
    w_f = jnp.pad(fox_w_in[0][:, 4 * D_ATT:], ((0, 0), (0, LANES - N_HEADS))).astype(BF16)
    b_f = jnp.pad(fox_b_f[0], (0, LANES - N_HEADS)).reshape(1, LANES).astype(F32)
    q_gain = (fox_q_gain[0] * (HEAD_DIM ** -0.5 * LOG2E)).reshape(HEAD_DIM, 1).astype(F32)
    k_gain = fox_k_gain[0].reshape(HEAD_DIM, 1).astype(F32)

    qt, k, vt, sg, c = _fox_in(x0, row(mix_pre_gain[0]), w_main, w_f, b_f, q_gain, k_gain)
    o = _attention(qt, k, vt, c)
    x2, u = _layer0_tail(o, sg, x0, fox_w_out[0].astype(BF16), row(mix_post_gain[0]),
                         ffn_weights(0), row(mix_pre_gain[1]), s5_w_in[0].astype(BF16))

    gps = (N_GROUPS, 1, STATE)
    a_re, a_im, bd_b_re, bd_b_im, bd_c_re, bd_c_im = _s5_discretise(
        s5_log_dt[0].reshape(N_GROUPS, 1, 1).astype(F32),
        s5_lam_re[0].reshape(gps).astype(F32), s5_lam_im[0].reshape(gps).astype(F32),
        s5_b_re[0].transpose(0, 2, 1).astype(F32), s5_b_im[0].transpose(0, 2, 1).astype(F32),
        s5_c_re[0].astype(F32), s5_c_im[0].astype(F32))
    y = _s5_scan(u, bd_b_re, bd_b_im, bd_c_re, bd_c_im, a_re, a_im, row(s5_d[0]))
    x4 = _layer1_tail(y.reshape(N_TOK, D_MODEL), x2, s5_w_glu[0].astype(BF16),
                      s5_w_out[0].astype(BF16), row(mix_post_gain[1]), ffn_weights(1))
    return x4.reshape(BATCH, SEQ, D_MODEL)
```

```python
import math

import jax
import jax.numpy as jnp
from jax import lax
from jax.experimental import pallas as pl
from jax.experimental.pallas import tpu as pltpu

F32 = jnp.float32
BF16 = jnp.bfloat16

D_MODEL = 1024
BATCH = 16
SEQ = 2048
N_TOK = BATCH * SEQ
N_HEADS = 16
HEAD_DIM = 64
D_ATT = N_HEADS * HEAD_DIM
SSM_GROUP = 16
N_GROUPS = 64
STATE = 64
D_FF = 2816
EPS = 1e-6

LANES = 128
ROW_TILE = 512
TILES_PER_SEQ = SEQ // ROW_TILE
SUB_TILES = (slice(0, ROW_TILE // 2), slice(ROW_TILE // 2, ROW_TILE))
ATT_TQ = 256
ATT_LOOKAHEAD = 2
ATT_ONES_ROWS = 16
LOG2E = math.log2(math.e)
HEAD_PAIRS = N_HEADS // 2
SCAN_STEPS = 32
SCAN_ROWS = SCAN_STEPS * BATCH
PERM_STEPS = 16
PERM_ROWS = PERM_STEPS * BATCH
GROUP_BLOCK = 16
N_GROUP_BLOCKS = N_GROUPS // GROUP_BLOCK
GB_CH = GROUP_BLOCK * SSM_GROUP
GB_ST = GROUP_BLOCK * STATE
MXU_DIM = 256
_FF_SPLIT = (D_FF // MXU_DIM // 2) * MXU_DIM
FF_CHUNKS = ((0, _FF_SPLIT), (_FF_SPLIT, D_FF))
VMEM_LIMIT = 56 * 1024 * 1024


def _const_spec(shape):
    nd = len(shape)
    return pl.BlockSpec(shape, lambda *_: (0,) * nd, pipeline_mode=pl.Buffered(1))


def _rmsnorm(xf, gain):
    return xf * lax.rsqrt(jnp.mean(xf * xf, axis=-1, keepdims=True) + EPS) * gain


def _dot(a, b):
    return jnp.dot(a, b, preferred_element_type=F32)


def _sigmoid(x):
    return 1.0 / (1.0 + jnp.exp(-x))


def _params(*sem):
    return pltpu.CompilerParams(dimension_semantics=sem, vmem_limit_bytes=VMEM_LIMIT)


def _fox_in_kernel(x_ref, g_ref, w_ref, wf_ref, bf_ref, qg_ref, kg_ref,
                   wg32_ref, wu32_ref, wd32_ref,
                   qt_ref, k_ref, vt_ref, sg_ref, c_ref, wg_ref, wu_ref, wd_ref, carry_ref):
    i = pl.program_id(0)
    hs = [_rmsnorm(x_ref[r, :], g_ref[...]).astype(BF16) for r in SUB_TILES]

    z = jnp.concatenate([_dot(h, wf_ref[...]) for h in hs], axis=0) + bf_ref[...]
    log_f = jnp.minimum(z, 0.0) - jnp.log(1.0 + jnp.exp(-jnp.abs(z)))
    rows = lax.broadcasted_iota(jnp.int32, (ROW_TILE, ROW_TILE), 0)
    cols = lax.broadcasted_iota(jnp.int32, (ROW_TILE, ROW_TILE), 1)
    tril = (rows >= cols).astype(BF16)
    hi = log_f.astype(BF16).astype(F32)
    r1 = log_f - hi
    mid = r1.astype(BF16).astype(F32)
    lo3 = r1 - mid
    lane = lax.broadcasted_iota(jnp.int32, (ROW_TILE, LANES), 1)
    packed = jnp.where(lane < N_HEADS, hi,
                       jnp.where(lane < 2 * N_HEADS, pltpu.roll(mid, N_HEADS, 1),
                                 pltpu.roll(lo3, 2 * N_HEADS, 1)))
    sums = _dot(tril, packed.astype(BF16))
    cum = (sums + pltpu.roll(sums, LANES - N_HEADS, 1)
           + pltpu.roll(sums, LANES - 2 * N_HEADS, 1))

    def head_norm_t(raw_t, gain_ref):
        out = []
        for hd in range(0, D_ATT, HEAD_DIM):
            rows = raw_t[hd:hd + HEAD_DIM]
            ssq = jnp.sum(rows * rows, axis=0, keepdims=True)
            out.append(rows * lax.rsqrt(ssq * (1.0 / HEAD_DIM) + EPS) * gain_ref[...])
        return jnp.concatenate(out, axis=0)

    q_raws = [_dot(h, w_ref[:, :D_ATT]) for h in hs]
    k_raws = [_dot(h, w_ref[:, D_ATT:2 * D_ATT]) for h in hs]
    wg_ref[...] = wg32_ref[...].astype(BF16)
    wu_ref[...] = wu32_ref[...].astype(BF16)
    wd_ref[...] = wd32_ref[...].astype(BF16)
    for r, q_raw in zip(SUB_TILES, q_raws):
        qt_ref[:, r] = head_norm_t(q_raw.T, qg_ref).astype(BF16)
    for r, h in zip(SUB_TILES, hs):
        vt_ref[:, r] = _dot(h, w_ref[:, 2 * D_ATT:3 * D_ATT]).T.astype(BF16)
    for r, k_raw in zip(SUB_TILES, k_raws):
        k_ref[r, :] = head_norm_t(k_raw.T, kg_ref).T.astype(BF16)
    for r, h in zip(SUB_TILES, hs):
        sg_ref[r, :] = _sigmoid(_dot(h, w_ref[:, 3 * D_ATT:4 * D_ATT])).astype(BF16)

    @pl.when(i % TILES_PER_SEQ == 0)
    def _():
        carry_ref[...] = jnp.zeros_like(carry_ref)

    cum = cum + carry_ref[...]
    c_ref[...] = cum * LOG2E
    carry_ref[...] = cum[ROW_TILE - 1:ROW_TILE, :]


def _fox_in(x, pre_gain, w_main, w_f, b_f, q_gain, k_gain, ffn_w_gate, ffn_w_up, ffn_w_down):
    n_steps = N_TOK // ROW_TILE
    row_spec = pl.BlockSpec((ROW_TILE, D_MODEL), lambda i: (i, 0))
    col_spec = pl.BlockSpec((D_ATT, ROW_TILE), lambda i: (0, i))
    act = jax.ShapeDtypeStruct((N_TOK, D_ATT), BF16)
    act_t = jax.ShapeDtypeStruct((D_ATT, N_TOK), BF16)
    gu_rows = D_MODEL // n_steps
    gu_spec = pl.BlockSpec((2, gu_rows, D_FF), lambda i: (0, i, 0))
    down_blocks = n_steps // 4
    down_rows = D_FF // down_blocks
    down_spec = pl.BlockSpec((None, down_rows, D_MODEL),
                             lambda i: (i // (2 * down_blocks), (i // 2) % down_blocks, 0))
    assert gu_rows % 16 == 0 and down_rows % 16 == 0 and down_rows * down_blocks == D_FF
    return pl.pallas_call(
        _fox_in_kernel,
        grid=(N_TOK // ROW_TILE,),
        in_specs=[
            row_spec,
            _const_spec((1, D_MODEL)),
            _layer_spec((D_MODEL, 4 * D_ATT), 0),
            _const_spec((D_MODEL, LANES)),
            _const_spec((1, LANES)),
            _const_spec((HEAD_DIM, 1)),
            _const_spec((HEAD_DIM, 1)),
            gu_spec, gu_spec, down_spec,
        ],
        out_specs=[col_spec, row_spec, col_spec, row_spec,
                   pl.BlockSpec((ROW_TILE, LANES), lambda i: (i, 0)),
                   gu_spec, gu_spec, down_spec],
        out_shape=[act_t, act, act_t, act, jax.ShapeDtypeStruct((N_TOK, LANES), F32),
                   jax.ShapeDtypeStruct(ffn_w_gate.shape, BF16),
                   jax.ShapeDtypeStruct(ffn_w_up.shape, BF16),
                   jax.ShapeDtypeStruct(ffn_w_down.shape, BF16)],
        scratch_shapes=[pltpu.VMEM((1, LANES), F32)],
        compiler_params=_params("arbitrary"),
        name="fox_in",
    )(x, pre_gain, w_main, w_f, b_f, q_gain, k_gain, ffn_w_gate, ffn_w_up, ffn_w_down)


def _attn_kernel(qt_ref, k_ref, vt_ref, c_ref, o_ref):
    pair = pl.program_id(1)
    k = k_ref[...]
    qt = qt_ref[...].astype(F32)
    low_rows = lax.broadcasted_iota(jnp.int32, (LANES, SEQ), 0) < HEAD_DIM
    qmt = (jnp.where(low_rows, qt, 0.0).astype(BF16), jnp.where(low_rows, 0.0, qt).astype(BF16))
    ones = jnp.ones((ATT_ONES_ROWS, SEQ), BF16)
    vmt = tuple(jnp.concatenate([vt_ref[hh * HEAD_DIM:(hh + 1) * HEAD_DIM, :], ones], axis=0)
                for hh in range(2))
    c_all = c_ref[...]
    lane = lax.broadcasted_iota(jnp.int32, (SEQ, LANES), 1)
    cb = tuple(
        jnp.broadcast_to(
            jnp.sum(jnp.where(lane == 2 * pair + hh, c_all, 0.0), axis=-1, keepdims=True),
            (SEQ, ATT_TQ))
        for hh in range(2))
    key = lax.broadcasted_iota(jnp.int32, (ATT_TQ, ATT_TQ), 0)
    qry = lax.broadcasted_iota(jnp.int32, (ATT_TQ, ATT_TQ), 1)
    causal = key <= qry

    def scores(qi, hh):
        k0 = qi * ATT_TQ
        qcols = qmt[hh][:, k0:k0 + ATT_TQ]
        s_d = _dot(k[k0:k0 + ATT_TQ], qcols)
        s_o = _dot(k[:k0], qcols) if qi else None
        return s_d, s_o

    def attend(qi, hh, s_d, s_o):
        k0 = qi * ATT_TQ
        s_d = jnp.where(causal, s_d - cb[hh][k0:k0 + ATT_TQ], -1e30)
        m = jnp.max(s_d, axis=0, keepdims=True)
        if qi:
            s_o = s_o - cb[hh][:k0]
            m = jnp.maximum(m, jnp.max(s_o, axis=0, keepdims=True))
        acc = _dot(vmt[hh][:, k0:k0 + ATT_TQ], jnp.exp2(s_d - m).astype(BF16))
        if qi:
            acc = acc + _dot(vmt[hh][:, :k0], jnp.exp2(s_o - m).astype(BF16))
        return acc[:HEAD_DIM] / acc[HEAD_DIM:HEAD_DIM + 1]

    items = [(qi, hh) for qi in range(SEQ // ATT_TQ) for hh in range(2)]
    ahead = [scores(*it) for it in items[:ATT_LOOKAHEAD]]
    outs = []
    for n, (qi, hh) in enumerate(items):
        current = ahead.pop(0)
        if n + ATT_LOOKAHEAD < len(items):
            ahead.append(scores(*items[n + ATT_LOOKAHEAD]))
        outs.append(attend(qi, hh, *current))
        if hh == 1:
            k0 = qi * ATT_TQ
            o_ref[k0:k0 + ATT_TQ, :] = jnp.concatenate(outs, axis=0).T.astype(BF16)
            outs = []


def _attention(qt, k, vt, c):
    seq_spec = pl.BlockSpec((SEQ, LANES), lambda b, j: (b, j))
    seq_t_spec = pl.BlockSpec((LANES, SEQ), lambda b, j: (j, b))
    return pl.pallas_call(
        _attn_kernel,
        grid=(BATCH, HEAD_PAIRS),
        in_specs=[seq_t_spec, seq_spec, seq_t_spec,
                  pl.BlockSpec((SEQ, LANES), lambda b, j: (b, 0))],
        out_specs=seq_spec,
        out_shape=jax.ShapeDtypeStruct((N_TOK, D_ATT), BF16),
        compiler_params=_params("arbitrary", "arbitrary"),
        name="fox_attention",
    )(qt, k, vt, c)


def _ffn_body(xs, pre_ref, wg_ref, wu_ref, wd_ref, post_ref):
    hs = [_rmsnorm(x, pre_ref[...]).astype(BF16) for x in xs]
    accs = [None] * len(xs)
    for c0, c1 in FF_CHUNKS:
        for n, h in enumerate(hs):
            g = _dot(h, wg_ref[:, c0:c1])
            u = _dot(h, wu_ref[:, c0:c1])
            a = (g * _sigmoid(g) * u).astype(BF16)
            part = _dot(a, wd_ref[c0:c1, :])
            accs[n] = part if accs[n] is None else accs[n] + part
    return [x + _rmsnorm(acc, post_ref[...]) for x, acc in zip(xs, accs)]


def _row_spec():
    return pl.BlockSpec((ROW_TILE, D_MODEL), lambda i: (i, 0))


def _layer_spec(shape, layer):
    nd = len(shape)
    return pl.BlockSpec((None,) + shape, lambda *_: (layer,) + (0,) * nd,
                        pipeline_mode=pl.Buffered(1))


def _ffn_specs(layer):
    return [_const_spec((1, D_MODEL)), _layer_spec((D_MODEL, D_FF), layer),
            _layer_spec((D_MODEL, D_FF), layer), _layer_spec((D_FF, D_MODEL), layer),
            _const_spec((1, D_MODEL))]


def _layer0_tail_kernel(o_ref, sg_ref, x_ref, wo_ref, mpost_ref,
                        fpre_ref, wg_ref, wu_ref, wd_ref, fpost_ref,
                        spre_ref, wsin_ref, x2_ref, u_ref):
    ms = [_dot(o_ref[r, :] * sg_ref[r, :], wo_ref[...]) for r in SUB_TILES]
    x1s = [x_ref[r, :] + _rmsnorm(m, mpost_ref[...]) for r, m in zip(SUB_TILES, ms)]
    x2s = _ffn_body(x1s, fpre_ref, wg_ref, wu_ref, wd_ref, fpost_ref)
    for r, x2 in zip(SUB_TILES, x2s):
        x2_ref[r, :] = x2
        u_ref[r, :] = _dot(_rmsnorm(x2, spre_ref[...]).astype(BF16), wsin_ref[...]).astype(BF16)


def _layer0_tail(o, sg, x, w_out, mix_post, ffn_w, s5_pre, s5_w_in):
    return pl.pallas_call(
        _layer0_tail_kernel,
        grid=(N_TOK // ROW_TILE,),
        in_specs=[_row_spec(), _row_spec(), _row_spec(),
                  _const_spec((D_ATT, D_MODEL)), _const_spec((1, D_MODEL)),
                  *_ffn_specs(0),
                  _const_spec((1, D_MODEL)), _const_spec((D_MODEL, D_MODEL))],
        out_specs=[_row_spec(), _row_spec()],
        out_shape=[jax.ShapeDtypeStruct((N_TOK, D_MODEL), F32),
                   jax.ShapeDtypeStruct((N_TOK, D_MODEL), BF16)],
        compiler_params=_params("arbitrary"),
        name="layer0_tail",
    )(o, sg, x, w_out, mix_post, *ffn_w, s5_pre, s5_w_in)


def _layer1_tail_kernel(y_ref, x_ref, wglu_ref, wo_ref, mpost_ref,
                        fpre_ref, wg_ref, wu_ref, wd_ref, fpost_ref, out_ref):
    ms = []
    for r in SUB_TILES:
        y = y_ref[r, :]
        gated = (y.astype(F32) * _sigmoid(_dot(y, wglu_ref[...]))).astype(BF16)
        ms.append(_dot(gated, wo_ref[...]))
    x3s = [x_ref[r, :] + _rmsnorm(m, mpost_ref[...]) for r, m in zip(SUB_TILES, ms)]
    outs = _ffn_body(x3s, fpre_ref, wg_ref, wu_ref, wd_ref, fpost_ref)
    for r, out in zip(SUB_TILES, outs):
        out_ref[r, :] = out


def _layer1_tail(y, x, w_glu, w_out, mix_post, ffn_w):
    return pl.pallas_call(
        _layer1_tail_kernel,
        grid=(N_TOK // ROW_TILE,),
        in_specs=[_row_spec(), _row_spec(),
                  _const_spec((D_MODEL, D_MODEL)), _const_spec((D_MODEL, D_MODEL)),
                  _const_spec((1, D_MODEL)), *_ffn_specs(1)],
        out_specs=_row_spec(),
        out_shape=jax.ShapeDtypeStruct((N_TOK, D_MODEL), F32),
        compiler_params=_params("arbitrary"),
        name="layer1_tail",
    )(y, x, w_glu, w_out, mix_post, *ffn_w)


def _s5_disc_kernel(ldt_ref, lre_ref, lim_ref, bre_ref, bim_ref, cre_ref, cim_ref,
                    are_ref, aim_ref, bdbre_ref, bdbim_ref, bdcre_ref, bdcim_ref,
                    cre_t_ref, cim_t_ref):
    dt = jnp.exp(ldt_ref[...])
    lam_re = lre_ref[...]
    lam_im = lim_ref[...]
    mag = jnp.exp(lam_re * dt)
    a_re = mag * jnp.cos(lam_im * dt)
    a_im = mag * jnp.sin(lam_im * dt)
    den = lam_re * lam_re + lam_im * lam_im
    n_re = a_re - 1.0
    z_re = (n_re * lam_re + a_im * lam_im) / den
    z_im = (a_im * lam_re - n_re * lam_im) / den
    b_re = bre_ref[...]
    b_im = bim_ref[...]
    planes = (
        (bdbre_ref, z_re * b_re - z_im * b_im),
        (bdbim_ref, z_re * b_im + z_im * b_re),
        (cre_t_ref, cre_ref[...]),
        (cim_t_ref, cim_ref[...]),
    )
    for dst, _ in planes:
        dst[...] = jnp.zeros_like(dst)
    zeros = jnp.zeros((SSM_GROUP, STATE), F32)
    for blk in range(N_GROUP_BLOCKS):
        for m in range(GROUP_BLOCK // 2):
            g0 = blk * GROUP_BLOCK + 2 * m
            rows = slice(2 * m * SSM_GROUP, 2 * (m + 1) * SSM_GROUP)
            cols = slice(2 * m * STATE, 2 * (m + 1) * STATE)
            for dst, src in planes:
                top = jnp.concatenate([src[g0], zeros], axis=1)
                bot = jnp.concatenate([zeros, src[g0 + 1]], axis=1)
                dst[blk, rows, cols] = jnp.concatenate([top, bot], axis=0).astype(dst.dtype)
            are_ref[blk, :, cols] = jnp.concatenate([a_re[g0], a_re[g0 + 1]], axis=1)
            aim_ref[blk, :, cols] = jnp.concatenate([a_im[g0], a_im[g0 + 1]], axis=1)
        bdcre_ref[blk] = cre_t_ref[blk].T.astype(BF16)
        bdcim_ref[blk] = cim_t_ref[blk].T.astype(BF16)


def _s5_discretise(log_dt, lam_re, lam_im, b_re_t, b_im_t, c_re, c_im):
    a_shape = jax.ShapeDtypeStruct((N_GROUP_BLOCKS, 1, GB_ST), F32)
    b_shape = jax.ShapeDtypeStruct((N_GROUP_BLOCKS, GB_CH, GB_ST), BF16)
    c_shape = jax.ShapeDtypeStruct((N_GROUP_BLOCKS, GB_ST, GB_CH), BF16)
    return pl.pallas_call(
        _s5_disc_kernel,
        out_shape=[a_shape, a_shape, b_shape, b_shape, c_shape, c_shape],
        scratch_shapes=[pltpu.VMEM((N_GROUP_BLOCKS, GB_CH, GB_ST), F32),
                        pltpu.VMEM((N_GROUP_BLOCKS, GB_CH, GB_ST), F32)],
        compiler_params=_params(),
        name="s5_discretise",
    )(log_dt, lam_re, lam_im, b_re_t, b_im_t, c_re, c_im)


def _s5_scan_kernel(u_ref, bre_ref, bim_ref, cre_ref, cim_ref, are_ref, aim_ref, d_ref,
                    y_ref, p_ref, pt_ref, xr_ref, xi_ref, sr_ref, si_ref):
    @pl.when(pl.program_id(0) == 0)
    def _():
        sr_ref[...] = jnp.zeros_like(sr_ref)
        si_ref[...] = jnp.zeros_like(si_ref)
        r = lax.broadcasted_iota(jnp.int32, (PERM_ROWS, PERM_ROWS), 0)
        c = lax.broadcasted_iota(jnp.int32, (PERM_ROWS, PERM_ROWS), 1)
        bits = BATCH.bit_length() - 1
        p_ref[...] = (c == (r & (BATCH - 1)) * PERM_STEPS + (r >> bits)).astype(BF16)
        pt_ref[...] = (r == (c & (BATCH - 1)) * PERM_STEPS + (c >> bits)).astype(BF16)

    u_f32 = jnp.concatenate(
        [_dot(p_ref[...], u_ref[:, t0:t0 + PERM_STEPS, :].reshape(PERM_ROWS, D_MODEL))
         for t0 in range(0, SCAN_STEPS, PERM_STEPS)], axis=0)
    u = u_f32.astype(BF16)

    def channels(g):
        return slice(g * GB_CH, (g + 1) * GB_CH)

    def drive(g):
        xr_ref[g] = _dot(u[:, channels(g)], bre_ref[g])
        xi_ref[g] = _dot(u[:, channels(g)], bim_ref[g])

    def recur(g):
        a_re = jnp.broadcast_to(are_ref[g], (BATCH, GB_ST))
        a_im = jnp.broadcast_to(aim_ref[g], (BATCH, GB_ST))
        s_re = sr_ref[g]
        s_im = si_ref[g]
        for t in range(SCAN_STEPS):
            rows = slice(t * BATCH, (t + 1) * BATCH)
            n_re = a_re * s_re - a_im * s_im + xr_ref[g, rows, :]
            n_im = a_re * s_im + a_im * s_re + xi_ref[g, rows, :]
            xr_ref[g, rows, :] = n_re
            xi_ref[g, rows, :] = n_im
            s_re, s_im = n_re, n_im
        sr_ref[g] = s_re
        si_ref[g] = s_im

    def readout(g):
        ch = channels(g)
        y = (_dot(xr_ref[g].astype(BF16), cre_ref[g])
             - _dot(xi_ref[g].astype(BF16), cim_ref[g]))
        y = y + d_ref[:, ch] * u_f32[:, ch]
        y = 0.5 * y * (1.0 + jnp.tanh(math.sqrt(2.0 / math.pi) * (y + 0.044715 * (y * y * y))))
        y = y.astype(BF16)
        for t0 in range(0, SCAN_STEPS, PERM_STEPS):
            y_bm = _dot(pt_ref[...], y[t0 * BATCH:(t0 + PERM_STEPS) * BATCH]).astype(BF16)
            y_ref[:, t0:t0 + PERM_STEPS, ch] = y_bm.reshape(BATCH, PERM_STEPS, GB_CH)

    drive(0)
    for g in range(N_GROUP_BLOCKS):
        if g + 1 < N_GROUP_BLOCKS:
            drive(g + 1)
        recur(g)
        readout(g)


def _s5_scan(u, b_re, b_im, c_re, c_im, a_re, a_im, d_skip):
    tile = pl.BlockSpec((BATCH, SCAN_STEPS, D_MODEL), lambda t: (0, t, 0))
    b_shape = (N_GROUP_BLOCKS, GB_CH, GB_ST)
    c_shape = (N_GROUP_BLOCKS, GB_ST, GB_CH)
    a_shape = (N_GROUP_BLOCKS, 1, GB_ST)
    return pl.pallas_call(
        _s5_scan_kernel,
        grid=(SEQ // SCAN_STEPS,),
        in_specs=[tile,
                  _const_spec(b_shape), _const_spec(b_shape),
                  _const_spec(c_shape), _const_spec(c_shape),
                  _const_spec(a_shape), _const_spec(a_shape),
                  _const_spec((1, D_MODEL))],
        out_specs=tile,
        out_shape=jax.ShapeDtypeStruct((BATCH, SEQ, D_MODEL), BF16),
        scratch_shapes=[pltpu.VMEM((PERM_ROWS, PERM_ROWS), BF16),
                        pltpu.VMEM((PERM_ROWS, PERM_ROWS), BF16),
                        pltpu.VMEM((N_GROUP_BLOCKS, SCAN_ROWS, GB_ST), F32),
                        pltpu.VMEM((N_GROUP_BLOCKS, SCAN_ROWS, GB_ST), F32),
                        pltpu.VMEM((N_GROUP_BLOCKS, BATCH, GB_ST), F32),
                        pltpu.VMEM((N_GROUP_BLOCKS, BATCH, GB_ST), F32)],
        compiler_params=_params("arbitrary"),
        name="s5_scan",
    )(u.reshape(BATCH, SEQ, D_MODEL), b_re, b_im, c_re, c_im, a_re, a_im, d_skip)


def kernel(x, fox_w_in, fox_b_f, fox_q_gain, fox_k_gain, fox_w_out, s5_w_in, s5_log_dt, s5_lam_re, s5_lam_im, s5_b_re, s5_b_im, s5_c_re, s5_c_im, s5_d, s5_w_glu, s5_w_out, mix_pre_gain, mix_post_gain, ffn_pre_gain, ffn_post_gain, ffn_w_gate, ffn_w_up, ffn_w_down):
    row = lambda a: a.reshape(1, -1).astype(F32)
    x0 = x.reshape(N_TOK, D_MODEL)

    w_main = fox_w_in.astype(BF16)
    w_f = jnp.pad(fox_w_in[0][:, 4 * D_ATT:], ((0, 0), (0, LANES - N_HEADS))).astype(BF16)
    b_f = jnp.pad(fox_b_f[0], (0, LANES - N_HEADS)).reshape(1, LANES).astype(F32)
    q_gain = (fox_q_gain[0] * (HEAD_DIM ** -0.5 * LOG2E)).reshape(HEAD_DIM, 1).astype(F32)
    k_gain = fox_k_gain[0].reshape(HEAD_DIM, 1).astype(F32)

    qt, k, vt, sg, c, w_gate, w_up, w_down = _fox_in(
        x0, row(mix_pre_gain[0]), w_main, w_f, b_f, q_gain, k_gain,
        ffn_w_gate, ffn_w_up, ffn_w_down)

    def ffn_weights(i):
        return (row(ffn_pre_gain[i]), w_gate, w_up, w_down, row(ffn_post_gain[i]))

    o = _attention(qt, k, vt, c)
    x2, u = _layer0_tail(o, sg, x0, fox_w_out[0].astype(BF16), row(mix_post_gain[0]),
                         ffn_weights(0), row(mix_pre_gain[1]), s5_w_in[0].astype(BF16))

    gps = (N_GROUPS, 1, STATE)
    a_re, a_im, bd_b_re, bd_b_im, bd_c_re, bd_c_im = _s5_discretise(
        s5_log_dt[0].reshape(N_GROUPS, 1, 1).astype(F32),
        s5_lam_re[0].reshape(gps).astype(F32), s5_lam_im[0].reshape(gps).astype(F32),
        s5_b_re[0].transpose(0, 2, 1).astype(F32), s5_b_im[0].transpose(0, 2, 1).astype(F32),
        s5_c_re[0].astype(F32), s5_c_im[0].astype(F32))
    y = _s5_scan(u, bd_b_re, bd_b_im, bd_c_re, bd_c_im, a_re, a_im, row(s5_d[0]))
    x4 = _layer1_tail(y.reshape(N_TOK, D_MODEL), x2, s5_w_glu[0].astype(BF16),
                      s5_w_out[0].astype(BF16), row(mix_post_gain[1]), ffn_weights(1))
    return x4.reshape(BATCH, SEQ, D_MODEL)
```

```python
import math

import jax
import jax.numpy as jnp
from jax import lax
from jax.experimental import pallas as pl
from jax.experimental.pallas import tpu as pltpu

F32 = jnp.float32
BF16 = jnp.bfloat16

D_MODEL = 1024
BATCH = 16
SEQ = 2048
N_TOK = BATCH * SEQ
N_HEADS = 16
HEAD_DIM = 64
D_ATT = N_HEADS * HEAD_DIM
SSM_GROUP = 16
N_GROUPS = 64
STATE = 64
D_FF = 2816
EPS = 1e-6

LANES = 128
ROW_TILE = 512
TILES_PER_SEQ = SEQ // ROW_TILE
SUB_TILES = (slice(0, ROW_TILE // 2), slice(ROW_TILE // 2, ROW_TILE))
ATT_TQ = 256
ATT_LOOKAHEAD = 2
ATT_ONES_ROWS = 16
LOG2E = math.log2(math.e)
HEAD_PAIRS = N_HEADS // 2
ATT_PAIRS = 2
ATT_HEADS = 2 * ATT_PAIRS
SCAN_STEPS = 32
SCAN_ROWS = SCAN_STEPS * BATCH
PERM_STEPS = 16
PERM_ROWS = PERM_STEPS * BATCH
GROUP_BLOCK = 16
N_GROUP_BLOCKS = N_GROUPS // GROUP_BLOCK
GB_CH = GROUP_BLOCK * SSM_GROUP
GB_ST = GROUP_BLOCK * STATE
MXU_DIM = 256
_FF_SPLIT = (D_FF // MXU_DIM // 2) * MXU_DIM
FF_CHUNKS = ((0, _FF_SPLIT), (_FF_SPLIT, D_FF))
VMEM_LIMIT = 56 * 1024 * 1024


def _const_spec(shape):
    nd = len(shape)
    return pl.BlockSpec(shape, lambda *_: (0,) * nd, pipeline_mode=pl.Buffered(1))


def _rmsnorm(xf, gain):
    return xf * lax.rsqrt(jnp.mean(xf * xf, axis=-1, keepdims=True) + EPS) * gain


def _dot(a, b):
    return jnp.dot(a, b, preferred_element_type=F32)


def _sigmoid(x):
    return 1.0 / (1.0 + jnp.exp(-x))


def _params(*sem):
    return pltpu.CompilerParams(dimension_semantics=sem, vmem_limit_bytes=VMEM_LIMIT)


def _fox_in_kernel(x_ref, g_ref, w_ref, wf_ref, bf_ref, qg_ref, kg_ref,
                   wg32_ref, wu32_ref, wd32_ref,
                   qt_ref, k_ref, vt_ref, sg_ref, c_ref, wg_ref, wu_ref, wd_ref, carry_ref):
    i = pl.program_id(0)
    hs = [_rmsnorm(x_ref[r, :], g_ref[...]).astype(BF16) for r in SUB_TILES]

    z = jnp.concatenate([_dot(h, wf_ref[...]) for h in hs], axis=0) + bf_ref[...]
    log_f = jnp.minimum(z, 0.0) - jnp.log(1.0 + jnp.exp(-jnp.abs(z)))
    rows = lax.broadcasted_iota(jnp.int32, (ROW_TILE, ROW_TILE), 0)
    cols = lax.broadcasted_iota(jnp.int32, (ROW_TILE, ROW_TILE), 1)
    tril = (rows >= cols).astype(BF16)
    hi = log_f.astype(BF16).astype(F32)
    r1 = log_f - hi
    mid = r1.astype(BF16).astype(F32)
    lo3 = r1 - mid
    lane = lax.broadcasted_iota(jnp.int32, (ROW_TILE, LANES), 1)
    packed = jnp.where(lane < N_HEADS, hi,
                       jnp.where(lane < 2 * N_HEADS, pltpu.roll(mid, N_HEADS, 1),
                                 pltpu.roll(lo3, 2 * N_HEADS, 1)))
    sums = _dot(tril, packed.astype(BF16))
    cum = (sums + pltpu.roll(sums, LANES - N_HEADS, 1)
           + pltpu.roll(sums, LANES - 2 * N_HEADS, 1))

    def head_norm_t(raw_t, gain_ref):
        out = []
        for hd in range(0, D_ATT, HEAD_DIM):
            rows = raw_t[hd:hd + HEAD_DIM]
            ssq = jnp.sum(rows * rows, axis=0, keepdims=True)
            out.append(rows * lax.rsqrt(ssq * (1.0 / HEAD_DIM) + EPS) * gain_ref[...])
        return jnp.concatenate(out, axis=0)

    q_raws = [_dot(h, w_ref[:, :D_ATT]) for h in hs]
    k_raws = [_dot(h, w_ref[:, D_ATT:2 * D_ATT]) for h in hs]
    wg_ref[...] = wg32_ref[...].astype(BF16)
    wu_ref[...] = wu32_ref[...].astype(BF16)
    wd_ref[...] = wd32_ref[...].astype(BF16)
    for r, q_raw in zip(SUB_TILES, q_raws):
        qt_ref[:, r] = head_norm_t(q_raw.T, qg_ref).astype(BF16)
    for r, h in zip(SUB_TILES, hs):
        vt_ref[:, r] = _dot(h, w_ref[:, 2 * D_ATT:3 * D_ATT]).T.astype(BF16)
    for r, k_raw in zip(SUB_TILES, k_raws):
        k_ref[r, :] = head_norm_t(k_raw.T, kg_ref).T.astype(BF16)
    for r, h in zip(SUB_TILES, hs):
        sg_ref[r, :] = _sigmoid(_dot(h, w_ref[:, 3 * D_ATT:4 * D_ATT])).astype(BF16)

    @pl.when(i % TILES_PER_SEQ == 0)
    def _():
        carry_ref[...] = jnp.zeros_like(carry_ref)

    cum = cum + carry_ref[...]
    c_ref[...] = cum * LOG2E
    carry_ref[...] = cum[ROW_TILE - 1:ROW_TILE, :]


def _fox_in(x, pre_gain, w_main, w_f, b_f, q_gain, k_gain, ffn_w_gate, ffn_w_up, ffn_w_down):
    n_steps = N_TOK // ROW_TILE
    row_spec = pl.BlockSpec((ROW_TILE, D_MODEL), lambda i: (i, 0))
    col_spec = pl.BlockSpec((D_ATT, ROW_TILE), lambda i: (0, i))
    act = jax.ShapeDtypeStruct((N_TOK, D_ATT), BF16)
    act_t = jax.ShapeDtypeStruct((D_ATT, N_TOK), BF16)
    gu_rows = D_MODEL // n_steps
    gu_spec = pl.BlockSpec((2, gu_rows, D_FF), lambda i: (0, i, 0))
    down_blocks = n_steps // 4
    down_rows = D_FF // down_blocks
    down_spec = pl.BlockSpec((None, down_rows, D_MODEL),
                             lambda i: (i // (2 * down_blocks), (i // 2) % down_blocks, 0))
    assert gu_rows % 16 == 0 and down_rows % 16 == 0 and down_rows * down_blocks == D_FF
    return pl.pallas_call(
        _fox_in_kernel,
        grid=(N_TOK // ROW_TILE,),
        in_specs=[
            row_spec,
            _const_spec((1, D_MODEL)),
            _layer_spec((D_MODEL, 4 * D_ATT), 0),
            _const_spec((D_MODEL, LANES)),
            _const_spec((1, LANES)),
            _const_spec((HEAD_DIM, 1)),
            _const_spec((HEAD_DIM, 1)),
            gu_spec, gu_spec, down_spec,
        ],
        out_specs=[col_spec, row_spec, col_spec, row_spec,
                   pl.BlockSpec((ROW_TILE, LANES), lambda i: (i, 0)),
                   gu_spec, gu_spec, down_spec],
        out_shape=[act_t, act, act_t, act, jax.ShapeDtypeStruct((N_TOK, LANES), F32),
                   jax.ShapeDtypeStruct(ffn_w_gate.shape, BF16),
                   jax.ShapeDtypeStruct(ffn_w_up.shape, BF16),
                   jax.ShapeDtypeStruct(ffn_w_down.shape, BF16)],
        scratch_shapes=[pltpu.VMEM((1, LANES), F32)],
        compiler_params=_params("arbitrary"),
        name="fox_in",
    )(x, pre_gain, w_main, w_f, b_f, q_gain, k_gain, ffn_w_gate, ffn_w_up, ffn_w_down)


def _attn_kernel(qt_ref, k_ref, vt_ref, c_ref, o_ref):
    first_head = ATT_HEADS * pl.program_id(1)
    low_rows = lax.broadcasted_iota(jnp.int32, (LANES, SEQ), 0) < HEAD_DIM
    ones = jnp.ones((ATT_ONES_ROWS, SEQ), BF16)
    c_all = c_ref[...]
    lane = lax.broadcasted_iota(jnp.int32, (SEQ, LANES), 1)
    ks, qmt, vmt, cb = [], [], [], []
    for pr in range(ATT_PAIRS):
        feat = slice(pr * LANES, (pr + 1) * LANES)
        ks.append(k_ref[:, feat])
        qt = qt_ref[feat, :].astype(F32)
        qmt += [jnp.where(low_rows, qt, 0.0).astype(BF16), jnp.where(low_rows, 0.0, qt).astype(BF16)]
    for hd in range(ATT_HEADS):
        vmt.append(jnp.concatenate([vt_ref[hd * HEAD_DIM:(hd + 1) * HEAD_DIM, :], ones], axis=0))
        col = jnp.sum(jnp.where(lane == first_head + hd, c_all, 0.0), axis=-1, keepdims=True)
        cb.append(jnp.broadcast_to(col, (SEQ, ATT_TQ)))
    key = lax.broadcasted_iota(jnp.int32, (ATT_TQ, ATT_TQ), 0)
    qry = lax.broadcasted_iota(jnp.int32, (ATT_TQ, ATT_TQ), 1)
    causal = key <= qry

    def scores(qi, hd):
        k0 = qi * ATT_TQ
        k = ks[hd // 2]
        qcols = qmt[hd][:, k0:k0 + ATT_TQ]
        s_d = _dot(k[k0:k0 + ATT_TQ], qcols)
        s_o = _dot(k[:k0], qcols) if qi else None
        return s_d, s_o

    def attend(qi, hd, s_d, s_o):
        k0 = qi * ATT_TQ
        s_d = jnp.where(causal, s_d - cb[hd][k0:k0 + ATT_TQ], -1e30)
        m = jnp.max(s_d, axis=0, keepdims=True)
        if qi:
            s_o = s_o - cb[hd][:k0]
            m = jnp.maximum(m, jnp.max(s_o, axis=0, keepdims=True))
        acc = _dot(vmt[hd][:, k0:k0 + ATT_TQ], jnp.exp2(s_d - m).astype(BF16))
        if qi:
            acc = acc + _dot(vmt[hd][:, :k0], jnp.exp2(s_o - m).astype(BF16))
        return acc[:HEAD_DIM] / acc[HEAD_DIM:HEAD_DIM + 1]

    items = [(qi, 2 * pr + hh) for pr in range(ATT_PAIRS)
             for qi in range(SEQ // ATT_TQ) for hh in range(2)]
    ahead = [scores(*it) for it in items[:ATT_LOOKAHEAD]]
    outs = []
    for n, (qi, hd) in enumerate(items):
        current = ahead.pop(0)
        if n + ATT_LOOKAHEAD < len(items):
            ahead.append(scores(*items[n + ATT_LOOKAHEAD]))
        outs.append(attend(qi, hd, *current))
        if hd % 2:
            k0 = qi * ATT_TQ
            feat = slice((hd // 2) * LANES, (hd // 2 + 1) * LANES)
            o_ref[k0:k0 + ATT_TQ, feat] = jnp.concatenate(outs, axis=0).T.astype(BF16)
            outs = []


def _attention(qt, k, vt, c):
    width = ATT_PAIRS * LANES
    seq_spec = pl.BlockSpec((SEQ, width), lambda b, j: (b, j))
    seq_t_spec = pl.BlockSpec((width, SEQ), lambda b, j: (j, b))
    return pl.pallas_call(
        _attn_kernel,
        grid=(BATCH, HEAD_PAIRS // ATT_PAIRS),
        in_specs=[seq_t_spec, seq_spec, seq_t_spec,
                  pl.BlockSpec((SEQ, LANES), lambda b, j: (b, 0))],
        out_specs=seq_spec,
        out_shape=jax.ShapeDtypeStruct((N_TOK, D_ATT), BF16),
        compiler_params=_params("arbitrary", "arbitrary"),
        name="fox_attention",
    )(qt, k, vt, c)


def _ffn_body(xs, pre_ref, wg_ref, wu_ref, wd_ref, post_ref):
    hs = [_rmsnorm(x, pre_ref[...]).astype(BF16) for x in xs]
    accs = [None] * len(xs)
    for c0, c1 in FF_CHUNKS:
        for n, h in enumerate(hs):
            g = _dot(h, wg_ref[:, c0:c1])
            u = _dot(h, wu_ref[:, c0:c1])
            a = (g * _sigmoid(g) * u).astype(BF16)
            part = _dot(a, wd_ref[c0:c1, :])
            accs[n] = part if accs[n] is None else accs[n] + part
    return [x + _rmsnorm(acc, post_ref[...]) for x, acc in zip(xs, accs)]


def _row_spec():
    return pl.BlockSpec((ROW_TILE, D_MODEL), lambda i: (i, 0))


def _layer_spec(shape, layer):
    nd = len(shape)
    return pl.BlockSpec((None,) + shape, lambda *_: (layer,) + (0,) * nd,
                        pipeline_mode=pl.Buffered(1))


def _ffn_specs(layer):
    return [_const_spec((1, D_MODEL)), _layer_spec((D_MODEL, D_FF), layer),
            _layer_spec((D_MODEL, D_FF), layer), _layer_spec((D_FF, D_MODEL), layer),
            _const_spec((1, D_MODEL))]


def _layer0_tail_kernel(o_ref, sg_ref, x_ref, wo_ref, mpost_ref,
                        fpre_ref, wg_ref, wu_ref, wd_ref, fpost_ref,
                        spre_ref, wsin_ref, x2_ref, u_ref):
    ms = [_dot(o_ref[r, :] * sg_ref[r, :], wo_ref[...]) for r in SUB_TILES]
    x1s = [x_ref[r, :] + _rmsnorm(m, mpost_ref[...]) for r, m in zip(SUB_TILES, ms)]
    x2s = _ffn_body(x1s, fpre_ref, wg_ref, wu_ref, wd_ref, fpost_ref)
    for r, x2 in zip(SUB_TILES, x2s):
        x2_ref[r, :] = x2
        u_ref[r, :] = _dot(_rmsnorm(x2, spre_ref[...]).astype(BF16), wsin_ref[...]).astype(BF16)


def _layer0_tail(o, sg, x, w_out, mix_post, ffn_w, s5_pre, s5_w_in):
    return pl.pallas_call(
        _layer0_tail_kernel,
        grid=(N_TOK // ROW_TILE,),
        in_specs=[_row_spec(), _row_spec(), _row_spec(),
                  _const_spec((D_ATT, D_MODEL)), _const_spec((1, D_MODEL)),
                  *_ffn_specs(0),
                  _const_spec((1, D_MODEL)), _const_spec((D_MODEL, D_MODEL))],
        out_specs=[_row_spec(), _row_spec()],
        out_shape=[jax.ShapeDtypeStruct((N_TOK, D_MODEL), F32),
                   jax.ShapeDtypeStruct((N_TOK, D_MODEL), BF16)],
        compiler_params=_params("arbitrary"),
        name="layer0_tail",
    )(o, sg, x, w_out, mix_post, *ffn_w, s5_pre, s5_w_in)


def _layer1_tail_kernel(y_ref, x_ref, wglu_ref, wo_ref, mpost_ref,
                        fpre_ref, wg_ref, wu_ref, wd_ref, fpost_ref, out_ref):
    ms = []
    for r in SUB_TILES:
        y = y_ref[r, :]
        gated = (y.astype(F32) * _sigmoid(_dot(y, wglu_ref[...]))).astype(BF16)
        ms.append(_dot(gated, wo_ref[...]))
    x3s = [x_ref[r, :] + _rmsnorm(m, mpost_ref[...]) for r, m in zip(SUB_TILES, ms)]
    outs = _ffn_body(x3s, fpre_ref, wg_ref, wu_ref, wd_ref, fpost_ref)
    for r, out in zip(SUB_TILES, outs):
        out_ref[r, :] = out


def _layer1_tail(y, x, w_glu, w_out, mix_post, ffn_w):
    return pl.pallas_call(
        _layer1_tail_kernel,
        grid=(N_TOK // ROW_TILE,),
        in_specs=[_row_spec(), _row_spec(),
                  _const_spec((D_MODEL, D_MODEL)), _const_spec((D_MODEL, D_MODEL)),
                  _const_spec((1, D_MODEL)), *_ffn_specs(1)],
        out_specs=_row_spec(),
        out_shape=jax.ShapeDtypeStruct((N_TOK, D_MODEL), F32),
        compiler_params=_params("arbitrary"),
        name="layer1_tail",
    )(y, x, w_glu, w_out, mix_post, *ffn_w)


def _s5_disc_kernel(ldt_ref, lre_ref, lim_ref, bre_ref, bim_ref, cre_ref, cim_ref,
                    are_ref, aim_ref, bdbre_ref, bdbim_ref, bdcre_ref, bdcim_ref,
                    cre_t_ref, cim_t_ref):
    dt = jnp.exp(ldt_ref[...])
    lam_re = lre_ref[...]
    lam_im = lim_ref[...]
    mag = jnp.exp(lam_re * dt)
    a_re = mag * jnp.cos(lam_im * dt)
    a_im = mag * jnp.sin(lam_im * dt)
    den = lam_re * lam_re + lam_im * lam_im
    n_re = a_re - 1.0
    z_re = (n_re * lam_re + a_im * lam_im) / den
    z_im = (a_im * lam_re - n_re * lam_im) / den
    b_re = bre_ref[...]
    b_im = bim_ref[...]
    planes = (
        (bdbre_ref, z_re * b_re - z_im * b_im),
        (bdbim_ref, z_re * b_im + z_im * b_re),
        (cre_t_ref, cre_ref[...]),
        (cim_t_ref, cim_ref[...]),
    )
    for dst, _ in planes:
        dst[...] = jnp.zeros_like(dst)
    zeros = jnp.zeros((SSM_GROUP, STATE), F32)
    for blk in range(N_GROUP_BLOCKS):
        for m in range(GROUP_BLOCK // 2):
            g0 = blk * GROUP_BLOCK + 2 * m
            rows = slice(2 * m * SSM_GROUP, 2 * (m + 1) * SSM_GROUP)
            cols = slice(2 * m * STATE, 2 * (m + 1) * STATE)
            for dst, src in planes:
                top = jnp.concatenate([src[g0], zeros], axis=1)
                bot = jnp.concatenate([zeros, src[g0 + 1]], axis=1)
                dst[blk, rows, cols] = jnp.concatenate([top, bot], axis=0).astype(dst.dtype)
            are_ref[blk, :, cols] = jnp.concatenate([a_re[g0], a_re[g0 + 1]], axis=1)
            aim_ref[blk, :, cols] = jnp.concatenate([a_im[g0], a_im[g0 + 1]], axis=1)
        bdcre_ref[blk] = cre_t_ref[blk].T.astype(BF16)
        bdcim_ref[blk] = cim_t_ref[blk].T.astype(BF16)


def _s5_discretise(log_dt, lam_re, lam_im, b_re_t, b_im_t, c_re, c_im):
    a_shape = jax.ShapeDtypeStruct((N_GROUP_BLOCKS, 1, GB_ST), F32)
    b_shape = jax.ShapeDtypeStruct((N_GROUP_BLOCKS, GB_CH, GB_ST), BF16)
    c_shape = jax.ShapeDtypeStruct((N_GROUP_BLOCKS, GB_ST, GB_CH), BF16)
    return pl.pallas_call(
        _s5_disc_kernel,
        out_shape=[a_shape, a_shape, b_shape, b_shape, c_shape, c_shape],
        scratch_shapes=[pltpu.VMEM((N_GROUP_BLOCKS, GB_CH, GB_ST), F32),
                        pltpu.VMEM((N_GROUP_BLOCKS, GB_CH, GB_ST), F32)],
        compiler_params=_params(),
        name="s5_discretise",
    )(log_dt, lam_re, lam_im, b_re_t, b_im_t, c_re, c_im)


def _s5_scan_kernel(u_ref, bre_ref, bim_ref, cre_ref, cim_ref, are_ref, aim_ref, d_ref,
                    y_ref, p_ref, pt_ref, xr_ref, xi_ref, sr_ref, si_ref):
    @pl.when(pl.program_id(0) == 0)
    def _():
        sr_ref[...] = jnp.zeros_like(sr_ref)
        si_ref[...] = jnp.zeros_like(si_ref)
        r = lax.broadcasted_iota(jnp.int32, (PERM_ROWS, PERM_ROWS), 0)
        c = lax.broadcasted_iota(jnp.int32, (PERM_ROWS, PERM_ROWS), 1)
        bits = BATCH.bit_length() - 1
        p_ref[...] = (c == (r & (BATCH - 1)) * PERM_STEPS + (r >> bits)).astype(BF16)
        pt_ref[...] = (r == (c & (BATCH - 1)) * PERM_STEPS + (c >> bits)).astype(BF16)

    u_f32 = jnp.concatenate(
        [_dot(p_ref[...], u_ref[:, t0:t0 + PERM_STEPS, :].reshape(PERM_ROWS, D_MODEL))
         for t0 in range(0, SCAN_STEPS, PERM_STEPS)], axis=0)
    u = u_f32.astype(BF16)

    def channels(g):
        return slice(g * GB_CH, (g + 1) * GB_CH)

    def drive(g):
        xr_ref[g] = _dot(u[:, channels(g)], bre_ref[g])
        xi_ref[g] = _dot(u[:, channels(g)], bim_ref[g])

    def recur(g):
        a_re = jnp.broadcast_to(are_ref[g], (BATCH, GB_ST))
        a_im = jnp.broadcast_to(aim_ref[g], (BATCH, GB_ST))
        s_re = sr_ref[g]
        s_im = si_ref[g]
        for t in range(SCAN_STEPS):
            rows = slice(t * BATCH, (t + 1) * BATCH)
            n_re = a_re * s_re - a_im * s_im + xr_ref[g, rows, :]
            n_im = a_re * s_im + a_im * s_re + xi_ref[g, rows, :]
            xr_ref[g, rows, :] = n_re
            xi_ref[g, rows, :] = n_im
            s_re, s_im = n_re, n_im
        sr_ref[g] = s_re
        si_ref[g] = s_im

    def readout(g):
        ch = channels(g)
        y = (_dot(xr_ref[g].astype(BF16), cre_ref[g])
             - _dot(xi_ref[g].astype(BF16), cim_ref[g]))
        y = y + d_ref[:, ch] * u_f32[:, ch]
        y = 0.5 * y * (1.0 + jnp.tanh(math.sqrt(2.0 / math.pi) * (y + 0.044715 * (y * y * y))))
        y = y.astype(BF16)
        for t0 in range(0, SCAN_STEPS, PERM_STEPS):
            y_bm = _dot(pt_ref[...], y[t0 * BATCH:(t0 + PERM_STEPS) * BATCH]).astype(BF16)
            y_ref[:, t0:t0 + PERM_STEPS, ch] = y_bm.reshape(BATCH, PERM_STEPS, GB_CH)

    drive(0)
    for g in range(N_GROUP_BLOCKS):
        if g + 1 < N_GROUP_BLOCKS:
            drive(g + 1)
        recur(g)
        readout(g)


def _s5_scan(u, b_re, b_im, c_re, c_im, a_re, a_im, d_skip):
    tile = pl.BlockSpec((BATCH, SCAN_STEPS, D_MODEL), lambda t: (0, t, 0))
    b_shape = (N_GROUP_BLOCKS, GB_CH, GB_ST)
    c_shape = (N_GROUP_BLOCKS, GB_ST, GB_CH)
    a_shape = (N_GROUP_BLOCKS, 1, GB_ST)
    return pl.pallas_call(
        _s5_scan_kernel,
        grid=(SEQ // SCAN_STEPS,),
        in_specs=[tile,
                  _const_spec(b_shape), _const_spec(b_shape),
                  _const_spec(c_shape), _const_spec(c_shape),
                  _const_spec(a_shape), _const_spec(a_shape),
                  _const_spec((1, D_MODEL))],
        out_specs=tile,
        out_shape=jax.ShapeDtypeStruct((BATCH, SEQ, D_MODEL), BF16),
        scratch_shapes=[pltpu.VMEM((PERM_ROWS, PERM_ROWS), BF16),
                        pltpu.VMEM((PERM_ROWS, PERM_ROWS), BF16),
                        pltpu.VMEM((N_GROUP_BLOCKS, SCAN_ROWS, GB_ST), F32),
                        pltpu.VMEM((N_GROUP_BLOCKS, SCAN_ROWS, GB_ST), F32),
                        pltpu.VMEM((N_GROUP_BLOCKS, BATCH, GB_ST), F32),
                        pltpu.VMEM((N_GROUP_BLOCKS, BATCH, GB_ST), F32)],
        compiler_params=_params("arbitrary"),
        name="s5_scan",
    )(u.reshape(BATCH, SEQ, D_MODEL), b_re, b_im, c_re, c_im, a_re, a_im, d_skip)


def kernel(x, fox_w_in, fox_b_f, fox_q_gain, fox_k_gain, fox_w_out, s5_w_in, s5_log_dt, s5_lam_re, s5_lam_im, s5_b_re, s5_b_im, s5_c_re, s5_c_im, s5_d, s5_w_glu, s5_w_out, mix_pre_gain, mix_post_gain, ffn_pre_gain, ffn_post_gain, ffn_w_gate, ffn_w_up, ffn_w_down):
    row = lambda a: a.reshape(1, -1).astype(F32)
    x0 = x.reshape(N_TOK, D_MODEL)

    w_main = fox_w_in.astype(BF16)
    w_f = jnp.pad(fox_w_in[0][:, 4 * D_ATT:], ((0, 0), (0, LANES - N_HEADS))).astype(BF16)
    b_f = jnp.pad(fox_b_f[0], (0, LANES - N_HEADS)).reshape(1, LANES).astype(F32)
    q_gain = (fox_q_gain[0] * (HEAD_DIM ** -0.5 * LOG2E)).reshape(HEAD_DIM, 1).astype(F32)
    k_gain = fox_k_gain[0].reshape(HEAD_DIM, 1).astype(F32)

    qt, k, vt, sg, c, w_gate, w_up, w_down = _fox_in(
        x0, row(mix_pre_gain[0]), w_main, w_f, b_f, q_gain, k_gain,
        ffn_w_gate, ffn_w_up, ffn_w_down)

    def ffn_weights(i):
        return (row(ffn_pre_gain[i]), w_gate, w_up, w_down, row(ffn_post_gain[i]))

    o = _attention(qt, k, vt, c)
    x2, u = _layer0_tail(o, sg, x0, fox_w_out[0].astype(BF16), row(mix_post_gain[0]),
                         ffn_weights(0), row(mix_pre_gain[1]), s5_w_in[0].astype(BF16))

    gps = (N_GROUPS, 1, STATE)
    a_re, a_im, bd_b_re, bd_b_im, bd_c_re, bd_c_im = _s5_discretise(
        s5_log_dt[0].reshape(N_GROUPS, 1, 1).astype(F32),
        s5_lam_re[0].reshape(gps).astype(F32), s5_lam_im[0].reshape(gps).astype(F32),
        s5_b_re[0].transpose(0, 2, 1).astype(F32), s5_b_im[0].transpose(0, 2, 1).astype(F32),
        s5_c_re[0].astype(F32), s5_c_im[0].astype(F32))
    y = _s5_scan(u, bd_b_re, bd_b_im, bd_c_re, bd_c_im, a_re, a_im, row(s5_d[0]))
    x4 = _layer1_tail(y.reshape(N_TOK, D_MODEL), x2, s5_w_glu[0].astype(BF16),
                      s5_w_out[0].astype(BF16), row(mix_post_gain[1]), ffn_weights(1))
    return x4.reshape(BATCH, SEQ, D_MODEL)
```

```python
import math

import jax
import jax.numpy as jnp
from jax import lax
from jax.experimental import pallas as pl
from jax.experimental.pallas import tpu as pltpu

F32 = jnp.float32
BF16 = jnp.bfloat16

D_MODEL = 1024
BATCH = 16
SEQ = 2048
N_TOK = BATCH * SEQ
N_HEADS = 16
HEAD_DIM = 64
D_ATT = N_HEADS * HEAD_DIM
SSM_GROUP = 16
N_GROUPS = 64
STATE = 64
D_FF = 2816
EPS = 1e-6

LANES = 128
ROW_TILE = 512
TILES_PER_SEQ = SEQ // ROW_TILE
SUB_TILES = (slice(0, ROW_TILE // 2), slice(ROW_TILE // 2, ROW_TILE))
ATT_TQ = 256
ATT_LOOKAHEAD = 2
ATT_ONES_ROWS = 16
LOG2E = math.log2(math.e)
HEAD_PAIRS = N_HEADS // 2
ATT_PAIRS = 4
ATT_HEADS = 2 * ATT_PAIRS
SCAN_STEPS = 32
SCAN_ROWS = SCAN_STEPS * BATCH
PERM_STEPS = 16
PERM_ROWS = PERM_STEPS * BATCH
GROUP_BLOCK = 16
N_GROUP_BLOCKS = N_GROUPS // GROUP_BLOCK
GB_CH = GROUP_BLOCK * SSM_GROUP
GB_ST = GROUP_BLOCK * STATE
MXU_DIM = 256
_FF_SPLIT = (D_FF // MXU_DIM // 2) * MXU_DIM
FF_CHUNKS = ((0, _FF_SPLIT), (_FF_SPLIT, D_FF))
VMEM_LIMIT = 56 * 1024 * 1024


def _const_spec(shape):
    nd = len(shape)
    return pl.BlockSpec(shape, lambda *_: (0,) * nd, pipeline_mode=pl.Buffered(1))


def _rmsnorm(xf, gain):
    return xf * lax.rsqrt(jnp.mean(xf * xf, axis=-1, keepdims=True) + EPS) * gain


def _dot(a, b):
    return jnp.dot(a, b, preferred_element_type=F32)


def _sigmoid(x):
    return 1.0 / (1.0 + jnp.exp(-x))


def _params(*sem):
    return pltpu.CompilerParams(dimension_semantics=sem, vmem_limit_bytes=VMEM_LIMIT)


def _fox_in_kernel(x_ref, g_ref, w_ref, wf_ref, bf_ref, qg_ref, kg_ref,
                   wg32_ref, wu32_ref, wd32_ref,
                   qt_ref, k_ref, vt_ref, sg_ref, c_ref, wg_ref, wu_ref, wd_ref, carry_ref):
    i = pl.program_id(0)
    hs = [_rmsnorm(x_ref[r, :], g_ref[...]).astype(BF16) for r in SUB_TILES]

    z = jnp.concatenate([_dot(h, wf_ref[...]) for h in hs], axis=0) + bf_ref[...]
    log_f = jnp.minimum(z, 0.0) - jnp.log(1.0 + jnp.exp(-jnp.abs(z)))
    rows = lax.broadcasted_iota(jnp.int32, (ROW_TILE, ROW_TILE), 0)
    cols = lax.broadcasted_iota(jnp.int32, (ROW_TILE, ROW_TILE), 1)
    tril = (rows >= cols).astype(BF16)
    hi = log_f.astype(BF16).astype(F32)
    r1 = log_f - hi
    mid = r1.astype(BF16).astype(F32)
    lo3 = r1 - mid
    lane = lax.broadcasted_iota(jnp.int32, (ROW_TILE, LANES), 1)
    packed = jnp.where(lane < N_HEADS, hi,
                       jnp.where(lane < 2 * N_HEADS, pltpu.roll(mid, N_HEADS, 1),
                                 pltpu.roll(lo3, 2 * N_HEADS, 1)))
    sums = _dot(tril, packed.astype(BF16))
    cum = (sums + pltpu.roll(sums, LANES - N_HEADS, 1)
           + pltpu.roll(sums, LANES - 2 * N_HEADS, 1))

    def head_norm_t(raw_t, gain_ref):
        out = []
        for hd in range(0, D_ATT, HEAD_DIM):
            rows = raw_t[hd:hd + HEAD_DIM]
            ssq = jnp.sum(rows * rows, axis=0, keepdims=True)
            out.append(rows * lax.rsqrt(ssq * (1.0 / HEAD_DIM) + EPS) * gain_ref[...])
        return jnp.concatenate(out, axis=0)

    q_raws = [_dot(h, w_ref[:, :D_ATT]) for h in hs]
    k_raws = [_dot(h, w_ref[:, D_ATT:2 * D_ATT]) for h in hs]
    wg_ref[...] = wg32_ref[...].astype(BF16)
    wu_ref[...] = wu32_ref[...].astype(BF16)
    wd_ref[...] = wd32_ref[...].astype(BF16)
    for r, q_raw in zip(SUB_TILES, q_raws):
        qt_ref[:, r] = head_norm_t(q_raw.T, qg_ref).astype(BF16)
    for r, h in zip(SUB_TILES, hs):
        vt_ref[:, r] = _dot(h, w_ref[:, 2 * D_ATT:3 * D_ATT]).T.astype(BF16)
    for r, k_raw in zip(SUB_TILES, k_raws):
        k_ref[r, :] = head_norm_t(k_raw.T, kg_ref).T.astype(BF16)
    for r, h in zip(SUB_TILES, hs):
        sg_ref[r, :] = _sigmoid(_dot(h, w_ref[:, 3 * D_ATT:4 * D_ATT])).astype(BF16)

    @pl.when(i % TILES_PER_SEQ == 0)
    def _():
        carry_ref[...] = jnp.zeros_like(carry_ref)

    cum = cum + carry_ref[...]
    c_ref[...] = cum * LOG2E
    carry_ref[...] = cum[ROW_TILE - 1:ROW_TILE, :]


def _fox_in(x, pre_gain, w_main, w_f, b_f, q_gain, k_gain, ffn_w_gate, ffn_w_up, ffn_w_down):
    n_steps = N_TOK // ROW_TILE
    row_spec = pl.BlockSpec((ROW_TILE, D_MODEL), lambda i: (i, 0))
    col_spec = pl.BlockSpec((D_ATT, ROW_TILE), lambda i: (0, i))
    act = jax.ShapeDtypeStruct((N_TOK, D_ATT), BF16)
    act_t = jax.ShapeDtypeStruct((D_ATT, N_TOK), BF16)
    gu_rows = D_MODEL // n_steps
    gu_spec = pl.BlockSpec((2, gu_rows, D_FF), lambda i: (0, i, 0))
    down_blocks = n_steps // 4
    down_rows = D_FF // down_blocks
    down_spec = pl.BlockSpec((None, down_rows, D_MODEL),
                             lambda i: (i // (2 * down_blocks), (i // 2) % down_blocks, 0))
    assert gu_rows % 16 == 0 and down_rows % 16 == 0 and down_rows * down_blocks == D_FF
    return pl.pallas_call(
        _fox_in_kernel,
        grid=(N_TOK // ROW_TILE,),
        in_specs=[
            row_spec,
            _const_spec((1, D_MODEL)),
            _layer_spec((D_MODEL, 4 * D_ATT), 0),
            _const_spec((D_MODEL, LANES)),
            _const_spec((1, LANES)),
            _const_spec((HEAD_DIM, 1)),
            _const_spec((HEAD_DIM, 1)),
            gu_spec, gu_spec, down_spec,
        ],
        out_specs=[col_spec, row_spec, col_spec, row_spec,
                   pl.BlockSpec((ROW_TILE, LANES), lambda i: (i, 0)),
                   gu_spec, gu_spec, down_spec],
        out_shape=[act_t, act, act_t, act, jax.ShapeDtypeStruct((N_TOK, LANES), F32),
                   jax.ShapeDtypeStruct(ffn_w_gate.shape, BF16),
                   jax.ShapeDtypeStruct(ffn_w_up.shape, BF16),
                   jax.ShapeDtypeStruct(ffn_w_down.shape, BF16)],
        scratch_shapes=[pltpu.VMEM((1, LANES), F32)],
        compiler_params=_params("arbitrary"),
        name="fox_in",
    )(x, pre_gain, w_main, w_f, b_f, q_gain, k_gain, ffn_w_gate, ffn_w_up, ffn_w_down)


def _attn_kernel(qt_ref, k_ref, vt_ref, c_ref, o_ref):
    first_head = ATT_HEADS * pl.program_id(1)
    low_rows = lax.broadcasted_iota(jnp.int32, (LANES, SEQ), 0) < HEAD_DIM
    ones = jnp.ones((ATT_ONES_ROWS, SEQ), BF16)
    c_all = c_ref[...]
    lane = lax.broadcasted_iota(jnp.int32, (SEQ, LANES), 1)
    ks, qmt, vmt, cb = [], [], [], []
    for pr in range(ATT_PAIRS):
        feat = slice(pr * LANES, (pr + 1) * LANES)
        ks.append(k_ref[:, feat])
        qt = qt_ref[feat, :].astype(F32)
        qmt += [jnp.where(low_rows, qt, 0.0).astype(BF16), jnp.where(low_rows, 0.0, qt).astype(BF16)]
    for hd in range(ATT_HEADS):
        vmt.append(jnp.concatenate([vt_ref[hd * HEAD_DIM:(hd + 1) * HEAD_DIM, :], ones], axis=0))
        col = jnp.sum(jnp.where(lane == first_head + hd, c_all, 0.0), axis=-1, keepdims=True)
        cb.append(jnp.broadcast_to(col, (SEQ, ATT_TQ)))
    key = lax.broadcasted_iota(jnp.int32, (ATT_TQ, ATT_TQ), 0)
    qry = lax.broadcasted_iota(jnp.int32, (ATT_TQ, ATT_TQ), 1)
    causal = key <= qry

    def scores(qi, hd):
        k0 = qi * ATT_TQ
        k = ks[hd // 2]
        qcols = qmt[hd][:, k0:k0 + ATT_TQ]
        s_d = _dot(k[k0:k0 + ATT_TQ], qcols)
        s_o = _dot(k[:k0], qcols) if qi else None
        return s_d, s_o

    def attend(qi, hd, s_d, s_o):
        k0 = qi * ATT_TQ
        s_d = jnp.where(causal, s_d - cb[hd][k0:k0 + ATT_TQ], -1e30)
        m = jnp.max(s_d, axis=0, keepdims=True)
        if qi:
            s_o = s_o - cb[hd][:k0]
            m = jnp.maximum(m, jnp.max(s_o, axis=0, keepdims=True))
        acc = _dot(vmt[hd][:, k0:k0 + ATT_TQ], jnp.exp2(s_d - m).astype(BF16))
        if qi:
            acc = acc + _dot(vmt[hd][:, :k0], jnp.exp2(s_o - m).astype(BF16))
        return acc[:HEAD_DIM] / acc[HEAD_DIM:HEAD_DIM + 1]

    items = [(qi, 2 * pr + hh) for pr in range(ATT_PAIRS)
             for qi in range(SEQ // ATT_TQ) for hh in range(2)]
    ahead = [scores(*it) for it in items[:ATT_LOOKAHEAD]]
    outs = []
    for n, (qi, hd) in enumerate(items):
        current = ahead.pop(0)
        if n + ATT_LOOKAHEAD < len(items):
            ahead.append(scores(*items[n + ATT_LOOKAHEAD]))
        outs.append(attend(qi, hd, *current))
        if hd % 2:
            k0 = qi * ATT_TQ
            feat = slice((hd // 2) * LANES, (hd // 2 + 1) * LANES)
            o_ref[k0:k0 + ATT_TQ, feat] = jnp.concatenate(outs, axis=0).T.astype(BF16)
            outs = []


def _attention(qt, k, vt, c):
    width = ATT_PAIRS * LANES
    seq_spec = pl.BlockSpec((SEQ, width), lambda b, j: (b, j))
    seq_t_spec = pl.BlockSpec((width, SEQ), lambda b, j: (j, b))
    return pl.pallas_call(
        _attn_kernel,
        grid=(BATCH, HEAD_PAIRS // ATT_PAIRS),
        in_specs=[seq_t_spec, seq_spec, seq_t_spec,
                  pl.BlockSpec((SEQ, LANES), lambda b, j: (b, 0))],
        out_specs=seq_spec,
        out_shape=jax.ShapeDtypeStruct((N_TOK, D_ATT), BF16),
        compiler_params=_params("arbitrary", "arbitrary"),
        name="fox_attention",
    )(qt, k, vt, c)


def _ffn_body(xs, pre_ref, wg_ref, wu_ref, wd_ref, post_ref):
    hs = [_rmsnorm(x, pre_ref[...]).astype(BF16) for x in xs]
    accs = [None] * len(xs)
    for c0, c1 in FF_CHUNKS:
        for n, h in enumerate(hs):
            g = _dot(h, wg_ref[:, c0:c1])
            u = _dot(h, wu_ref[:, c0:c1])
            a = (g * _sigmoid(g) * u).astype(BF16)
            part = _dot(a, wd_ref[c0:c1, :])
            accs[n] = part if accs[n] is None else accs[n] + part
    return [x + _rmsnorm(acc, post_ref[...]) for x, acc in zip(xs, accs)]


def _row_spec():
    return pl.BlockSpec((ROW_TILE, D_MODEL), lambda i: (i, 0))


def _layer_spec(shape, layer):
    nd = len(shape)
    return pl.BlockSpec((None,) + shape, lambda *_: (layer,) + (0,) * nd,
                        pipeline_mode=pl.Buffered(1))


def _ffn_specs(layer):
    return [_const_spec((1, D_MODEL)), _layer_spec((D_MODEL, D_FF), layer),
            _layer_spec((D_MODEL, D_FF), layer), _layer_spec((D_FF, D_MODEL), layer),
            _const_spec((1, D_MODEL))]


def _layer0_tail_kernel(o_ref, sg_ref, x_ref, wo_ref, mpost_ref,
                        fpre_ref, wg_ref, wu_ref, wd_ref, fpost_ref,
                        spre_ref, wsin_ref, x2_ref, u_ref):
    ms = [_dot(o_ref[r, :] * sg_ref[r, :], wo_ref[...]) for r in SUB_TILES]
    x1s = [x_ref[r, :] + _rmsnorm(m, mpost_ref[...]) for r, m in zip(SUB_TILES, ms)]
    x2s = _ffn_body(x1s, fpre_ref, wg_ref, wu_ref, wd_ref, fpost_ref)
    for r, x2 in zip(SUB_TILES, x2s):
        x2_ref[r, :] = x2
        u_ref[r, :] = _dot(_rmsnorm(x2, spre_ref[...]).astype(BF16), wsin_ref[...]).astype(BF16)


def _layer0_tail(o, sg, x, w_out, mix_post, ffn_w, s5_pre, s5_w_in):
    return pl.pallas_call(
        _layer0_tail_kernel,
        grid=(N_TOK // ROW_TILE,),
        in_specs=[_row_spec(), _row_spec(), _row_spec(),
                  _const_spec((D_ATT, D_MODEL)), _const_spec((1, D_MODEL)),
                  *_ffn_specs(0),
                  _const_spec((1, D_MODEL)), _const_spec((D_MODEL, D_MODEL))],
        out_specs=[_row_spec(), _row_spec()],
        out_shape=[jax.ShapeDtypeStruct((N_TOK, D_MODEL), F32),
                   jax.ShapeDtypeStruct((N_TOK, D_MODEL), BF16)],
        compiler_params=_params("arbitrary"),
        name="layer0_tail",
    )(o, sg, x, w_out, mix_post, *ffn_w, s5_pre, s5_w_in)


def _layer1_tail_kernel(y_ref, x_ref, wglu_ref, wo_ref, mpost_ref,
                        fpre_ref, wg_ref, wu_ref, wd_ref, fpost_ref, out_ref):
    ms = []
    for r in SUB_TILES:
        y = y_ref[r, :]
        gated = (y.astype(F32) * _sigmoid(_dot(y, wglu_ref[...]))).astype(BF16)
        ms.append(_dot(gated, wo_ref[...]))
    x3s = [x_ref[r, :] + _rmsnorm(m, mpost_ref[...]) for r, m in zip(SUB_TILES, ms)]
    outs = _ffn_body(x3s, fpre_ref, wg_ref, wu_ref, wd_ref, fpost_ref)
    for r, out in zip(SUB_TILES, outs):
        out_ref[r, :] = out


def _layer1_tail(y, x, w_glu, w_out, mix_post, ffn_w):
    return pl.pallas_call(
        _layer1_tail_kernel,
        grid=(N_TOK // ROW_TILE,),
        in_specs=[_row_spec(), _row_spec(),
                  _const_spec((D_MODEL, D_MODEL)), _const_spec((D_MODEL, D_MODEL)),
                  _const_spec((1, D_MODEL)), *_ffn_specs(1)],
        out_specs=_row_spec(),
        out_shape=jax.ShapeDtypeStruct((N_TOK, D_MODEL), F32),
        compiler_params=_params("arbitrary"),
        name="layer1_tail",
    )(y, x, w_glu, w_out, mix_post, *ffn_w)


def _s5_disc_kernel(ldt_ref, lre_ref, lim_ref, bre_ref, bim_ref, cre_ref, cim_ref,
                    are_ref, aim_ref, bdbre_ref, bdbim_ref, bdcre_ref, bdcim_ref,
                    cre_t_ref, cim_t_ref):
    dt = jnp.exp(ldt_ref[...])
    lam_re = lre_ref[...]
    lam_im = lim_ref[...]
    mag = jnp.exp(lam_re * dt)
    a_re = mag * jnp.cos(lam_im * dt)
    a_im = mag * jnp.sin(lam_im * dt)
    den = lam_re * lam_re + lam_im * lam_im
    n_re = a_re - 1.0
    z_re = (n_re * lam_re + a_im * lam_im) / den
    z_im = (a_im * lam_re - n_re * lam_im) / den
    b_re = bre_ref[...]
    b_im = bim_ref[...]
    planes = (
        (bdbre_ref, z_re * b_re - z_im * b_im),
        (bdbim_ref, z_re * b_im + z_im * b_re),
        (cre_t_ref, cre_ref[...]),
        (cim_t_ref, cim_ref[...]),
    )
    for dst, _ in planes:
        dst[...] = jnp.zeros_like(dst)
    zeros = jnp.zeros((SSM_GROUP, STATE), F32)
    for blk in range(N_GROUP_BLOCKS):
        for m in range(GROUP_BLOCK // 2):
            g0 = blk * GROUP_BLOCK + 2 * m
            rows = slice(2 * m * SSM_GROUP, 2 * (m + 1) * SSM_GROUP)
            cols = slice(2 * m * STATE, 2 * (m + 1) * STATE)
            for dst, src in planes:
                top = jnp.concatenate([src[g0], zeros], axis=1)
                bot = jnp.concatenate([zeros, src[g0 + 1]], axis=1)
                dst[blk, rows, cols] = jnp.concatenate([top, bot], axis=0).astype(dst.dtype)
            are_ref[blk, :, cols] = jnp.concatenate([a_re[g0], a_re[g0 + 1]], axis=1)
            aim_ref[blk, :, cols] = jnp.concatenate([a_im[g0], a_im[g0 + 1]], axis=1)
        bdcre_ref[blk] = cre_t_ref[blk].T.astype(BF16)
        bdcim_ref[blk] = cim_t_ref[blk].T.astype(BF16)


def _s5_discretise(log_dt, lam_re, lam_im, b_re_t, b_im_t, c_re, c_im):
    a_shape = jax.ShapeDtypeStruct((N_GROUP_BLOCKS, 1, GB_ST), F32)
    b_shape = jax.ShapeDtypeStruct((N_GROUP_BLOCKS, GB_CH, GB_ST), BF16)
    c_shape = jax.ShapeDtypeStruct((N_GROUP_BLOCKS, GB_ST, GB_CH), BF16)
    return pl.pallas_call(
        _s5_disc_kernel,
        out_shape=[a_shape, a_shape, b_shape, b_shape, c_shape, c_shape],
        scratch_shapes=[pltpu.VMEM((N_GROUP_BLOCKS, GB_CH, GB_ST), F32),
                        pltpu.VMEM((N_GROUP_BLOCKS, GB_CH, GB_ST), F32)],
        compiler_params=_params(),
        name="s5_discretise",
    )(log_dt, lam_re, lam_im, b_re_t, b_im_t, c_re, c_im)


def _s5_scan_kernel(u_ref, bre_ref, bim_ref, cre_ref, cim_ref, are_ref, aim_ref, d_ref,
                    y_ref, p_ref, pt_ref, xr_ref, xi_ref, sr_ref, si_ref):
    @pl.when(pl.program_id(0) == 0)
    def _():
        sr_ref[...] = jnp.zeros_like(sr_ref)
        si_ref[...] = jnp.zeros_like(si_ref)
        r = lax.broadcasted_iota(jnp.int32, (PERM_ROWS, PERM_ROWS), 0)
        c = lax.broadcasted_iota(jnp.int32, (PERM_ROWS, PERM_ROWS), 1)
        bits = BATCH.bit_length() - 1
        p_ref[...] = (c == (r & (BATCH - 1)) * PERM_STEPS + (r >> bits)).astype(BF16)
        pt_ref[...] = (r == (c & (BATCH - 1)) * PERM_STEPS + (c >> bits)).astype(BF16)

    u_f32 = jnp.concatenate(
        [_dot(p_ref[...], u_ref[:, t0:t0 + PERM_STEPS, :].reshape(PERM_ROWS, D_MODEL))
         for t0 in range(0, SCAN_STEPS, PERM_STEPS)], axis=0)
    u = u_f32.astype(BF16)

    def channels(g):
        return slice(g * GB_CH, (g + 1) * GB_CH)

    def drive(g):
        xr_ref[g] = _dot(u[:, channels(g)], bre_ref[g])
        xi_ref[g] = _dot(u[:, channels(g)], bim_ref[g])

    def recur(g):
        a_re = jnp.broadcast_to(are_ref[g], (BATCH, GB_ST))
        a_im = jnp.broadcast_to(aim_ref[g], (BATCH, GB_ST))
        s_re = sr_ref[g]
        s_im = si_ref[g]
        for t in range(SCAN_STEPS):
            rows = slice(t * BATCH, (t + 1) * BATCH)
            n_re = a_re * s_re - a_im * s_im + xr_ref[g, rows, :]
            n_im = a_re * s_im + a_im * s_re + xi_ref[g, rows, :]
            xr_ref[g, rows, :] = n_re
            xi_ref[g, rows, :] = n_im
            s_re, s_im = n_re, n_im
        sr_ref[g] = s_re
        si_ref[g] = s_im

    def readout(g):
        ch = channels(g)
        y = (_dot(xr_ref[g].astype(BF16), cre_ref[g])
             - _dot(xi_ref[g].astype(BF16), cim_ref[g]))
        y = y + d_ref[:, ch] * u_f32[:, ch]
        y = 0.5 * y * (1.0 + jnp.tanh(math.sqrt(2.0 / math.pi) * (y + 0.044715 * (y * y * y))))
        y = y.astype(BF16)
        for t0 in range(0, SCAN_STEPS, PERM_STEPS):
            y_bm = _dot(pt_ref[...], y[t0 * BATCH:(t0 + PERM_STEPS) * BATCH]).astype(BF16)
            y_ref[:, t0:t0 + PERM_STEPS, ch] = y_bm.reshape(BATCH, PERM_STEPS, GB_CH)

    drive(0)
    for g in range(N_GROUP_BLOCKS):
        if g + 1 < N_GROUP_BLOCKS:
            drive(g + 1)
        recur(g)
        readout(g)


def _s5_scan(u, b_re, b_im, c_re, c_im, a_re, a_im, d_skip):
    tile = pl.BlockSpec((BATCH, SCAN_STEPS, D_MODEL), lambda t: (0, t, 0))
    b_shape = (N_GROUP_BLOCKS, GB_CH, GB_ST)
    c_shape = (N_GROUP_BLOCKS, GB_ST, GB_CH)
    a_shape = (N_GROUP_BLOCKS, 1, GB_ST)
    return pl.pallas_call(
        _s5_scan_kernel,
        grid=(SEQ // SCAN_STEPS,),
        in_specs=[tile,
                  _const_spec(b_shape), _const_spec(b_shape),
                  _const_spec(c_shape), _const_spec(c_shape),
                  _const_spec(a_shape), _const_spec(a_shape),
                  _const_spec((1, D_MODEL))],
        out_specs=tile,
        out_shape=jax.ShapeDtypeStruct((BATCH, SEQ, D_MODEL), BF16),
        scratch_shapes=[pltpu.VMEM((PERM_ROWS, PERM_ROWS), BF16),
                        pltpu.VMEM((PERM_ROWS, PERM_ROWS), BF16),
                        pltpu.VMEM((N_GROUP_BLOCKS, SCAN_ROWS, GB_ST), F32),
                        pltpu.VMEM((N_GROUP_BLOCKS, SCAN_ROWS, GB_ST), F32),
                        pltpu.VMEM((N_GROUP_BLOCKS, BATCH, GB_ST), F32),
                        pltpu.VMEM((N_GROUP_BLOCKS, BATCH, GB_ST), F32)],
        compiler_params=_params("arbitrary"),
        name="s5_scan",
    )(u.reshape(BATCH, SEQ, D_MODEL), b_re, b_im, c_re, c_im, a_re, a_im, d_skip)


def kernel(x, fox_w_in, fox_b_f, fox_q_gain, fox_k_gain, fox_w_out, s5_w_in, s5_log_dt, s5_lam_re, s5_lam_im, s5_b_re, s5_b_im, s5_c_re, s5_c_im, s5_d, s5_w_glu, s5_w_out, mix_pre_gain, mix_post_gain, ffn_pre_gain, ffn_post_gain, ffn_w_gate, ffn_w_up, ffn_w_down):
    row = lambda a: a.reshape(1, -1).astype(F32)
    x0 = x.reshape(N_TOK, D_MODEL)

    w_main = fox_w_in.astype(BF16)
    w_f = jnp.pad(fox_w_in[0][:, 4 * D_ATT:], ((0, 0), (0, LANES - N_HEADS))).astype(BF16)
    b_f = jnp.pad(fox_b_f[0], (0, LANES - N_HEADS)).reshape(1, LANES).astype(F32)
    q_gain = (fox_q_gain[0] * (HEAD_DIM ** -0.5 * LOG2E)).reshape(HEAD_DIM, 1).astype(F32)
    k_gain = fox_k_gain[0].reshape(HEAD_DIM, 1).astype(F32)

    qt, k, vt, sg, c, w_gate, w_up, w_down = _fox_in(
        x0, row(mix_pre_gain[0]), w_main, w_f, b_f, q_gain, k_gain,
        ffn_w_gate, ffn_w_up, ffn_w_down)

    def ffn_weights(i):
        return (row(ffn_pre_gain[i]), w_gate, w_up, w_down, row(ffn_post_gain[i]))

    o = _attention(qt, k, vt, c)
    x2, u = _layer0_tail(o, sg, x0, fox_w_out[0].astype(BF16), row(mix_post_gain[0]),
                         ffn_weights(0), row(mix_pre_gain[1]), s5_w_in[0].astype(BF16))

    gps = (N_GROUPS, 1, STATE)
    a_re, a_im, bd_b_re, bd_b_im, bd_c_re, bd_c_im = _s5_discretise(
        s5_log_dt[0].reshape(N_GROUPS, 1, 1).astype(F32),
        s5_lam_re[0].reshape(gps).astype(F32), s5_lam_im[0].reshape(gps).astype(F32),
        s5_b_re[0].transpose(0, 2, 1).astype(F32), s5_b_im[0].transpose(0, 2, 1).astype(F32),
        s5_c_re[0].astype(F32), s5_c_im[0].astype(F32))
    y = _s5_scan(u, bd_b_re, bd_b_im, bd_c_re, bd_c_im, a_re, a_im, row(s5_d[0]))
    x4 = _layer1_tail(y.reshape(N_TOK, D_MODEL), x2, s5_w_glu[0].astype(BF16),
                      s5_w_out[0].astype(BF16), row(mix_post_gain[1]), ffn_weights(1))
    return x4.reshape(BATCH, SEQ, D_MODEL)
```

```python
import math

import jax
import jax.numpy as jnp
from jax import lax
from jax.experimental import pallas as pl
from jax.experimental.pallas import tpu as pltpu

F32 = jnp.float32
BF16 = jnp.bfloat16

D_MODEL = 1024
BATCH = 16
SEQ = 2048
N_TOK = BATCH * SEQ
N_HEADS = 16
HEAD_DIM = 64
D_ATT = N_HEADS * HEAD_DIM
SSM_GROUP = 16
N_GROUPS = 64
STATE = 64
D_FF = 2816
EPS = 1e-6

LANES = 128
ROW_TILE = 512
TILES_PER_SEQ = SEQ // ROW_TILE
SUB_TILES = (slice(0, ROW_TILE // 2), slice(ROW_TILE // 2, ROW_TILE))
TAIL1_ROWS = 2 * ROW_TILE
TAIL1_SUB_TILES = tuple(slice(r, r + ROW_TILE // 2) for r in range(0, TAIL1_ROWS, ROW_TILE // 2))
ATT_TQ = 256
ATT_LOOKAHEAD = 2
ATT_ONES_ROWS = 16
LOG2E = math.log2(math.e)
HEAD_PAIRS = N_HEADS // 2
ATT_PAIRS = 2
ATT_HEADS = 2 * ATT_PAIRS
SCAN_STEPS = 32
SCAN_ROWS = SCAN_STEPS * BATCH
PERM_STEPS = 16
PERM_ROWS = PERM_STEPS * BATCH
GROUP_BLOCK = 16
N_GROUP_BLOCKS = N_GROUPS // GROUP_BLOCK
GB_CH = GROUP_BLOCK * SSM_GROUP
GB_ST = GROUP_BLOCK * STATE
MXU_DIM = 256
_FF_SPLIT = (D_FF // MXU_DIM // 2) * MXU_DIM
FF_CHUNKS = ((0, _FF_SPLIT), (_FF_SPLIT, D_FF))
VMEM_LIMIT = 56 * 1024 * 1024


def _const_spec(shape):
    nd = len(shape)
    return pl.BlockSpec(shape, lambda *_: (0,) * nd, pipeline_mode=pl.Buffered(1))


def _rmsnorm(xf, gain):
    return xf * lax.rsqrt(jnp.mean(xf * xf, axis=-1, keepdims=True) + EPS) * gain


def _dot(a, b):
    return jnp.dot(a, b, preferred_element_type=F32)


def _sigmoid(x):
    return 1.0 / (1.0 + jnp.exp(-x))


def _params(*sem):
    return pltpu.CompilerParams(dimension_semantics=sem, vmem_limit_bytes=VMEM_LIMIT)


def _fox_in_kernel(x_ref, g_ref, w_ref, wf_ref, bf_ref, qg_ref, kg_ref,
                   wg32_ref, wu32_ref, wd32_ref,
                   qt_ref, k_ref, vt_ref, sg_ref, c_ref, wg_ref, wu_ref, wd_ref, carry_ref):
    i = pl.program_id(0)
    hs = [_rmsnorm(x_ref[r, :], g_ref[...]).astype(BF16) for r in SUB_TILES]

    z = jnp.concatenate([_dot(h, wf_ref[...]) for h in hs], axis=0) + bf_ref[...]
    log_f = jnp.minimum(z, 0.0) - jnp.log(1.0 + jnp.exp(-jnp.abs(z)))
    rows = lax.broadcasted_iota(jnp.int32, (ROW_TILE, ROW_TILE), 0)
    cols = lax.broadcasted_iota(jnp.int32, (ROW_TILE, ROW_TILE), 1)
    tril = (rows >= cols).astype(BF16)
    hi = log_f.astype(BF16).astype(F32)
    r1 = log_f - hi
    mid = r1.astype(BF16).astype(F32)
    lo3 = r1 - mid
    lane = lax.broadcasted_iota(jnp.int32, (ROW_TILE, LANES), 1)
    packed = jnp.where(lane < N_HEADS, hi,
                       jnp.where(lane < 2 * N_HEADS, pltpu.roll(mid, N_HEADS, 1),
                                 pltpu.roll(lo3, 2 * N_HEADS, 1)))
    sums = _dot(tril, packed.astype(BF16))
    cum = (sums + pltpu.roll(sums, LANES - N_HEADS, 1)
           + pltpu.roll(sums, LANES - 2 * N_HEADS, 1))

    def head_norm_t(raw_t, gain_ref):
        out = []
        for hd in range(0, D_ATT, HEAD_DIM):
            rows = raw_t[hd:hd + HEAD_DIM]
            ssq = jnp.sum(rows * rows, axis=0, keepdims=True)
            out.append(rows * lax.rsqrt(ssq * (1.0 / HEAD_DIM) + EPS) * gain_ref[...])
        return jnp.concatenate(out, axis=0)

    q_raws = [_dot(h, w_ref[:, :D_ATT]) for h in hs]
    k_raws = [_dot(h, w_ref[:, D_ATT:2 * D_ATT]) for h in hs]
    wg_ref[...] = wg32_ref[...].astype(BF16)
    wu_ref[...] = wu32_ref[...].astype(BF16)
    wd_ref[...] = wd32_ref[...].astype(BF16)
    for r, q_raw in zip(SUB_TILES, q_raws):
        qt_ref[:, r] = head_norm_t(q_raw.T, qg_ref).astype(BF16)
    for r, h in zip(SUB_TILES, hs):
        vt_ref[:, r] = _dot(h, w_ref[:, 2 * D_ATT:3 * D_ATT]).T.astype(BF16)
    for r, k_raw in zip(SUB_TILES, k_raws):
        k_ref[r, :] = head_norm_t(k_raw.T, kg_ref).T.astype(BF16)
    for r, h in zip(SUB_TILES, hs):
        sg_ref[r, :] = _sigmoid(_dot(h, w_ref[:, 3 * D_ATT:4 * D_ATT])).astype(BF16)

    @pl.when(i % TILES_PER_SEQ == 0)
    def _():
        carry_ref[...] = jnp.zeros_like(carry_ref)

    cum = cum + carry_ref[...]
    c_ref[...] = cum * LOG2E
    carry_ref[...] = cum[ROW_TILE - 1:ROW_TILE, :]


def _fox_in(x, pre_gain, w_main, w_f, b_f, q_gain, k_gain, ffn_w_gate, ffn_w_up, ffn_w_down):
    n_steps = N_TOK // ROW_TILE
    row_spec = pl.BlockSpec((ROW_TILE, D_MODEL), lambda i: (i, 0))
    col_spec = pl.BlockSpec((D_ATT, ROW_TILE), lambda i: (0, i))
    act = jax.ShapeDtypeStruct((N_TOK, D_ATT), BF16)
    act_t = jax.ShapeDtypeStruct((D_ATT, N_TOK), BF16)
    gu_rows = D_MODEL // n_steps
    gu_spec = pl.BlockSpec((2, gu_rows, D_FF), lambda i: (0, i, 0))
    down_blocks = n_steps // 4
    down_rows = D_FF // down_blocks
    down_spec = pl.BlockSpec((None, down_rows, D_MODEL),
                             lambda i: (i // (2 * down_blocks), (i // 2) % down_blocks, 0))
    assert gu_rows % 16 == 0 and down_rows % 16 == 0 and down_rows * down_blocks == D_FF
    return pl.pallas_call(
        _fox_in_kernel,
        grid=(N_TOK // ROW_TILE,),
        in_specs=[
            row_spec,
            _const_spec((1, D_MODEL)),
            _layer_spec((D_MODEL, 4 * D_ATT), 0),
            _const_spec((D_MODEL, LANES)),
            _const_spec((1, LANES)),
            _const_spec((HEAD_DIM, 1)),
            _const_spec((HEAD_DIM, 1)),
            gu_spec, gu_spec, down_spec,
        ],
        out_specs=[col_spec, row_spec, col_spec, row_spec,
                   pl.BlockSpec((ROW_TILE, LANES), lambda i: (i, 0)),
                   gu_spec, gu_spec, down_spec],
        out_shape=[act_t, act, act_t, act, jax.ShapeDtypeStruct((N_TOK, LANES), F32),
                   jax.ShapeDtypeStruct(ffn_w_gate.shape, BF16),
                   jax.ShapeDtypeStruct(ffn_w_up.shape, BF16),
                   jax.ShapeDtypeStruct(ffn_w_down.shape, BF16)],
        scratch_shapes=[pltpu.VMEM((1, LANES), F32)],
        compiler_params=_params("arbitrary"),
        name="fox_in",
    )(x, pre_gain, w_main, w_f, b_f, q_gain, k_gain, ffn_w_gate, ffn_w_up, ffn_w_down)


def _attn_kernel(qt_ref, k_ref, vt_ref, c_ref, o_ref):
    first_head = ATT_HEADS * pl.program_id(1)
    low_rows = lax.broadcasted_iota(jnp.int32, (LANES, SEQ), 0) < HEAD_DIM
    ones = jnp.ones((ATT_ONES_ROWS, SEQ), BF16)
    c_all = c_ref[...]
    lane = lax.broadcasted_iota(jnp.int32, (SEQ, LANES), 1)
    ks, qmt, vmt, cb = [], [], [], []
    for pr in range(ATT_PAIRS):
        feat = slice(pr * LANES, (pr + 1) * LANES)
        ks.append(k_ref[:, feat])
        qt = qt_ref[feat, :].astype(F32)
        qmt += [jnp.where(low_rows, qt, 0.0).astype(BF16), jnp.where(low_rows, 0.0, qt).astype(BF16)]
    for hd in range(ATT_HEADS):
        vmt.append(jnp.concatenate([vt_ref[hd * HEAD_DIM:(hd + 1) * HEAD_DIM, :], ones], axis=0))
        col = jnp.sum(jnp.where(lane == first_head + hd, c_all, 0.0), axis=-1, keepdims=True)
        cb.append(jnp.broadcast_to(col, (SEQ, ATT_TQ)))
    key = lax.broadcasted_iota(jnp.int32, (ATT_TQ, ATT_TQ), 0)
    qry = lax.broadcasted_iota(jnp.int32, (ATT_TQ, ATT_TQ), 1)
    causal = key <= qry

    def scores(qi, hd):
        k0 = qi * ATT_TQ
        k = ks[hd // 2]
        qcols = qmt[hd][:, k0:k0 + ATT_TQ]
        s_d = _dot(k[k0:k0 + ATT_TQ], qcols)
        s_o = _dot(k[:k0], qcols) if qi else None
        return s_d, s_o

    def attend(qi, hd, s_d, s_o):
        k0 = qi * ATT_TQ
        s_d = jnp.where(causal, s_d - cb[hd][k0:k0 + ATT_TQ], -1e30)
        m = jnp.max(s_d, axis=0, keepdims=True)
        if qi:
            s_o = s_o - cb[hd][:k0]
            m = jnp.maximum(m, jnp.max(s_o, axis=0, keepdims=True))
        acc = _dot(vmt[hd][:, k0:k0 + ATT_TQ], jnp.exp2(s_d - m).astype(BF16))
        if qi:
            acc = acc + _dot(vmt[hd][:, :k0], jnp.exp2(s_o - m).astype(BF16))
        return acc[:HEAD_DIM] / acc[HEAD_DIM:HEAD_DIM + 1]

    items = [(qi, 2 * pr + hh) for pr in range(ATT_PAIRS)
             for qi in range(SEQ // ATT_TQ) for hh in range(2)]
    ahead = [scores(*it) for it in items[:ATT_LOOKAHEAD]]
    outs = []
    for n, (qi, hd) in enumerate(items):
        current = ahead.pop(0)
        if n + ATT_LOOKAHEAD < len(items):
            ahead.append(scores(*items[n + ATT_LOOKAHEAD]))
        outs.append(attend(qi, hd, *current))
        if hd % 2:
            k0 = qi * ATT_TQ
            feat = slice((hd // 2) * LANES, (hd // 2 + 1) * LANES)
            o_ref[k0:k0 + ATT_TQ, feat] = jnp.concatenate(outs, axis=0).T.astype(BF16)
            outs = []


def _attention(qt, k, vt, c):
    width = ATT_PAIRS * LANES
    seq_spec = pl.BlockSpec((SEQ, width), lambda b, j: (b, j))
    seq_t_spec = pl.BlockSpec((width, SEQ), lambda b, j: (j, b))
    return pl.pallas_call(
        _attn_kernel,
        grid=(BATCH, HEAD_PAIRS // ATT_PAIRS),
        in_specs=[seq_t_spec, seq_spec, seq_t_spec,
                  pl.BlockSpec((SEQ, LANES), lambda b, j: (b, 0))],
        out_specs=seq_spec,
        out_shape=jax.ShapeDtypeStruct((N_TOK, D_ATT), BF16),
        compiler_params=_params("arbitrary", "arbitrary"),
        name="fox_attention",
    )(qt, k, vt, c)


def _ffn_body(xs, pre_ref, wg_ref, wu_ref, wd_ref, post_ref):
    hs = [_rmsnorm(x, pre_ref[...]).astype(BF16) for x in xs]
    accs = [None] * len(xs)
    for c0, c1 in FF_CHUNKS:
        for n, h in enumerate(hs):
            g = _dot(h, wg_ref[:, c0:c1])
            u = _dot(h, wu_ref[:, c0:c1])
            a = (g * _sigmoid(g) * u).astype(BF16)
            part = _dot(a, wd_ref[c0:c1, :])
            accs[n] = part if accs[n] is None else accs[n] + part
    return [x + _rmsnorm(acc, post_ref[...]) for x, acc in zip(xs, accs)]


def _row_spec():
    return pl.BlockSpec((ROW_TILE, D_MODEL), lambda i: (i, 0))


def _layer_spec(shape, layer):
    nd = len(shape)
    return pl.BlockSpec((None,) + shape, lambda *_: (layer,) + (0,) * nd,
                        pipeline_mode=pl.Buffered(1))


def _ffn_specs(layer):
    return [_const_spec((1, D_MODEL)), _layer_spec((D_MODEL, D_FF), layer),
            _layer_spec((D_MODEL, D_FF), layer), _layer_spec((D_FF, D_MODEL), layer),
            _const_spec((1, D_MODEL))]


def _layer0_tail_kernel(o_ref, sg_ref, x_ref, wo_ref, mpost_ref,
                        fpre_ref, wg_ref, wu_ref, wd_ref, fpost_ref,
                        spre_ref, wsin_ref, x2_ref, u_ref):
    ms = [_dot(o_ref[r, :] * sg_ref[r, :], wo_ref[...]) for r in SUB_TILES]
    x1s = [x_ref[r, :] + _rmsnorm(m, mpost_ref[...]) for r, m in zip(SUB_TILES, ms)]
    x2s = _ffn_body(x1s, fpre_ref, wg_ref, wu_ref, wd_ref, fpost_ref)
    for r, x2 in zip(SUB_TILES, x2s):
        x2_ref[r, :] = x2
        u_ref[r, :] = _dot(_rmsnorm(x2, spre_ref[...]).astype(BF16), wsin_ref[...]).astype(BF16)


def _layer0_tail(o, sg, x, w_out, mix_post, ffn_w, s5_pre, s5_w_in):
    return pl.pallas_call(
        _layer0_tail_kernel,
        grid=(N_TOK // ROW_TILE,),
        in_specs=[_row_spec(), _row_spec(), _row_spec(),
                  _const_spec((D_ATT, D_MODEL)), _const_spec((1, D_MODEL)),
                  *_ffn_specs(0),
                  _const_spec((1, D_MODEL)), _const_spec((D_MODEL, D_MODEL))],
        out_specs=[_row_spec(), _row_spec()],
        out_shape=[jax.ShapeDtypeStruct((N_TOK, D_MODEL), F32),
                   jax.ShapeDtypeStruct((N_TOK, D_MODEL), BF16)],
        compiler_params=_params("arbitrary"),
        name="layer0_tail",
    )(o, sg, x, w_out, mix_post, *ffn_w, s5_pre, s5_w_in)


def _layer1_tail_kernel(y_ref, x_ref, wglu_ref, wo_ref, mpost_ref,
                        fpre_ref, wg_ref, wu_ref, wd_ref, fpost_ref, out_ref):
    ms = []
    for r in TAIL1_SUB_TILES:
        y = y_ref[r, :]
        gated = (y.astype(F32) * _sigmoid(_dot(y, wglu_ref[...]))).astype(BF16)
        ms.append(_dot(gated, wo_ref[...]))
    x3s = [x_ref[r, :] + _rmsnorm(m, mpost_ref[...]) for r, m in zip(TAIL1_SUB_TILES, ms)]
    outs = _ffn_body(x3s, fpre_ref, wg_ref, wu_ref, wd_ref, fpost_ref)
    for r, out in zip(TAIL1_SUB_TILES, outs):
        out_ref[r, :] = out


def _layer1_tail(y, x, w_glu, w_out, mix_post, ffn_w):
    rows = pl.BlockSpec((TAIL1_ROWS, D_MODEL), lambda i: (i, 0))
    return pl.pallas_call(
        _layer1_tail_kernel,
        grid=(N_TOK // TAIL1_ROWS,),
        in_specs=[rows, rows,
                  _const_spec((D_MODEL, D_MODEL)), _const_spec((D_MODEL, D_MODEL)),
                  _const_spec((1, D_MODEL)), *_ffn_specs(1)],
        out_specs=rows,
        out_shape=jax.ShapeDtypeStruct((N_TOK, D_MODEL), F32),
        compiler_params=_params("arbitrary"),
        name="layer1_tail",
    )(y, x, w_glu, w_out, mix_post, *ffn_w)


def _s5_disc_kernel(ldt_ref, lre_ref, lim_ref, bre_ref, bim_ref, cre_ref, cim_ref,
                    are_ref, aim_ref, bdbre_ref, bdbim_ref, bdcre_ref, bdcim_ref,
                    cre_t_ref, cim_t_ref):
    dt = jnp.exp(ldt_ref[...])
    lam_re = lre_ref[...]
    lam_im = lim_ref[...]
    mag = jnp.exp(lam_re * dt)
    a_re = mag * jnp.cos(lam_im * dt)
    a_im = mag * jnp.sin(lam_im * dt)
    den = lam_re * lam_re + lam_im * lam_im
    n_re = a_re - 1.0
    z_re = (n_re * lam_re + a_im * lam_im) / den
    z_im = (a_im * lam_re - n_re * lam_im) / den
    b_re = bre_ref[...]
    b_im = bim_ref[...]
    planes = (
        (bdbre_ref, z_re * b_re - z_im * b_im),
        (bdbim_ref, z_re * b_im + z_im * b_re),
        (cre_t_ref, cre_ref[...]),
        (cim_t_ref, cim_ref[...]),
    )
    for dst, _ in planes:
        dst[...] = jnp.zeros_like(dst)
    zeros = jnp.zeros((SSM_GROUP, STATE), F32)
    for blk in range(N_GROUP_BLOCKS):
        for m in range(GROUP_BLOCK // 2):
            g0 = blk * GROUP_BLOCK + 2 * m
            rows = slice(2 * m * SSM_GROUP, 2 * (m + 1) * SSM_GROUP)
            cols = slice(2 * m * STATE, 2 * (m + 1) * STATE)
            for dst, src in planes:
                top = jnp.concatenate([src[g0], zeros], axis=1)
                bot = jnp.concatenate([zeros, src[g0 + 1]], axis=1)
                dst[blk, rows, cols] = jnp.concatenate([top, bot], axis=0).astype(dst.dtype)
            are_ref[blk, :, cols] = jnp.concatenate([a_re[g0], a_re[g0 + 1]], axis=1)
            aim_ref[blk, :, cols] = jnp.concatenate([a_im[g0], a_im[g0 + 1]], axis=1)
        bdcre_ref[blk] = cre_t_ref[blk].T.astype(BF16)
        bdcim_ref[blk] = cim_t_ref[blk].T.astype(BF16)


def _s5_discretise(log_dt, lam_re, lam_im, b_re_t, b_im_t, c_re, c_im):
    a_shape = jax.ShapeDtypeStruct((N_GROUP_BLOCKS, 1, GB_ST), F32)
    b_shape = jax.ShapeDtypeStruct((N_GROUP_BLOCKS, GB_CH, GB_ST), BF16)
    c_shape = jax.ShapeDtypeStruct((N_GROUP_BLOCKS, GB_ST, GB_CH), BF16)
    return pl.pallas_call(
        _s5_disc_kernel,
        out_shape=[a_shape, a_shape, b_shape, b_shape, c_shape, c_shape],
        scratch_shapes=[pltpu.VMEM((N_GROUP_BLOCKS, GB_CH, GB_ST), F32),
                        pltpu.VMEM((N_GROUP_BLOCKS, GB_CH, GB_ST), F32)],
        compiler_params=_params(),
        name="s5_discretise",
    )(log_dt, lam_re, lam_im, b_re_t, b_im_t, c_re, c_im)


def _s5_scan_kernel(u_ref, bre_ref, bim_ref, cre_ref, cim_ref, are_ref, aim_ref, d_ref,
                    y_ref, p_ref, pt_ref, xr_ref, xi_ref, sr_ref, si_ref):
    @pl.when(pl.program_id(0) == 0)
    def _():
        sr_ref[...] = jnp.zeros_like(sr_ref)
        si_ref[...] = jnp.zeros_like(si_ref)
        r = lax.broadcasted_iota(jnp.int32, (PERM_ROWS, PERM_ROWS), 0)
        c = lax.broadcasted_iota(jnp.int32, (PERM_ROWS, PERM_ROWS), 1)
        bits = BATCH.bit_length() - 1
        p_ref[...] = (c == (r & (BATCH - 1)) * PERM_STEPS + (r >> bits)).astype(BF16)
        pt_ref[...] = (r == (c & (BATCH - 1)) * PERM_STEPS + (c >> bits)).astype(BF16)

    u_f32 = jnp.concatenate(
        [_dot(p_ref[...], u_ref[:, t0:t0 + PERM_STEPS, :].reshape(PERM_ROWS, D_MODEL))
         for t0 in range(0, SCAN_STEPS, PERM_STEPS)], axis=0)
    u = u_f32.astype(BF16)

    def channels(g):
        return slice(g * GB_CH, (g + 1) * GB_CH)

    def drive(g):
        xr_ref[g] = _dot(u[:, channels(g)], bre_ref[g])
        xi_ref[g] = _dot(u[:, channels(g)], bim_ref[g])

    def recur(g):
        a_re = jnp.broadcast_to(are_ref[g], (BATCH, GB_ST))
        a_im = jnp.broadcast_to(aim_ref[g], (BATCH, GB_ST))
        s_re = sr_ref[g]
        s_im = si_ref[g]
        for t in range(SCAN_STEPS):
            rows = slice(t * BATCH, (t + 1) * BATCH)
            n_re = a_re * s_re - a_im * s_im + xr_ref[g, rows, :]
            n_im = a_re * s_im + a_im * s_re + xi_ref[g, rows, :]
            xr_ref[g, rows, :] = n_re
            xi_ref[g, rows, :] = n_im
            s_re, s_im = n_re, n_im
        sr_ref[g] = s_re
        si_ref[g] = s_im

    def readout(g):
        ch = channels(g)
        y = (_dot(xr_ref[g].astype(BF16), cre_ref[g])
             - _dot(xi_ref[g].astype(BF16), cim_ref[g]))
        y = y + d_ref[:, ch] * u_f32[:, ch]
        y = 0.5 * y * (1.0 + jnp.tanh(math.sqrt(2.0 / math.pi) * (y + 0.044715 * (y * y * y))))
        y = y.astype(BF16)
        for t0 in range(0, SCAN_STEPS, PERM_STEPS):
            y_bm = _dot(pt_ref[...], y[t0 * BATCH:(t0 + PERM_STEPS) * BATCH]).astype(BF16)
            y_ref[:, t0:t0 + PERM_STEPS, ch] = y_bm.reshape(BATCH, PERM_STEPS, GB_CH)

    drive(0)
    for g in range(N_GROUP_BLOCKS):
        if g + 1 < N_GROUP_BLOCKS:
            drive(g + 1)
        recur(g)
        readout(g)


def _s5_scan(u, b_re, b_im, c_re, c_im, a_re, a_im, d_skip):
    tile = pl.BlockSpec((BATCH, SCAN_STEPS, D_MODEL), lambda t: (0, t, 0))
    b_shape = (N_GROUP_BLOCKS, GB_CH, GB_ST)
    c_shape = (N_GROUP_BLOCKS, GB_ST, GB_CH)
    a_shape = (N_GROUP_BLOCKS, 1, GB_ST)
    return pl.pallas_call(
        _s5_scan_kernel,
        grid=(SEQ // SCAN_STEPS,),
        in_specs=[tile,
                  _const_spec(b_shape), _const_spec(b_shape),
                  _const_spec(c_shape), _const_spec(c_shape),
                  _const_spec(a_shape), _const_spec(a_shape),
                  _const_spec((1, D_MODEL))],
        out_specs=tile,
        out_shape=jax.ShapeDtypeStruct((BATCH, SEQ, D_MODEL), BF16),
        scratch_shapes=[pltpu.VMEM((PERM_ROWS, PERM_ROWS), BF16),
                        pltpu.VMEM((PERM_ROWS, PERM_ROWS), BF16),
                        pltpu.VMEM((N_GROUP_BLOCKS, SCAN_ROWS, GB_ST), F32),
                        pltpu.VMEM((N_GROUP_BLOCKS, SCAN_ROWS, GB_ST), F32),
                        pltpu.VMEM((N_GROUP_BLOCKS, BATCH, GB_ST), F32),
                        pltpu.VMEM((N_GROUP_BLOCKS, BATCH, GB_ST), F32)],
        compiler_params=_params("arbitrary"),
        name="s5_scan",
    )(u.reshape(BATCH, SEQ, D_MODEL), b_re, b_im, c_re, c_im, a_re, a_im, d_skip)


def kernel(x, fox_w_in, fox_b_f, fox_q_gain, fox_k_gain, fox_w_out, s5_w_in, s5_log_dt, s5_lam_re, s5_lam_im, s5_b_re, s5_b_im, s5_c_re, s5_c_im, s5_d, s5_w_glu, s5_w_out, mix_pre_gain, mix_post_gain, ffn_pre_gain, ffn_post_gain, ffn_w_gate, ffn_w_up, ffn_w_down):
    row = lambda a: a.reshape(1, -1).astype(F32)
    x0 = x.reshape(N_TOK, D_MODEL)

    w_main = fox_w_in.astype(BF16)
    w_f = jnp.pad(fox_w_in[0][:, 4 * D_ATT:], ((0, 0), (0, LANES - N_HEADS))).astype(BF16)
    b_f = jnp.pad(fox_b_f[0], (0, LANES - N_HEADS)).reshape(1, LANES).astype(F32)
    q_gain = (fox_q_gain[0] * (HEAD_DIM ** -0.5 * LOG2E)).reshape(HEAD_DIM, 1).astype(F32)
    k_gain = fox_k_gain[0].reshape(HEAD_DIM, 1).astype(F32)

    qt, k, vt, sg, c, w_gate, w_up, w_down = _fox_in(
        x0, row(mix_pre_gain[0]), w_main, w_f, b_f, q_gain, k_gain,
        ffn_w_gate, ffn_w_up, ffn_w_down)

    def ffn_weights(i):
        return (row(ffn_pre_gain[i]), w_gate, w_up, w_down, row(ffn_post_gain[i]))

    o = _attention(qt, k, vt, c)
    x2, u = _layer0_tail(o, sg, x0, fox_w_out[0].astype(BF16), row(mix_post_gain[0]),
                         ffn_weights(0), row(mix_pre_gain[1]), s5_w_in[0].astype(BF16))

    gps = (N_GROUPS, 1, STATE)
    a_re, a_im, bd_b_re, bd_b_im, bd_c_re, bd_c_im = _s5_discretise(
        s5_log_dt[0].reshape(N_GROUPS, 1, 1).astype(F32),
        s5_lam_re[0].reshape(gps).astype(F32), s5_lam_im[0].reshape(gps).astype(F32),
        s5_b_re[0].transpose(0, 2, 1).astype(F32), s5_b_im[0].transpose(0, 2, 1).astype(F32),
        s5_c_re[0].astype(F32), s5_c_im[0].astype(F32))
    y = _s5_scan(u, bd_b_re, bd_b_im, bd_c_re, bd_c_im, a_re, a_im, row(s5_d[0]))
    x4 = _layer1_tail(y.reshape(N_TOK, D_MODEL), x2, s5_w_glu[0].astype(BF16),
                      s5_w_out[0].astype(BF16), row(mix_post_gain[1]), ffn_weights(1))
    return x4.reshape(BATCH, SEQ, D_MODEL)
```

```python
import math

import jax
import jax.numpy as jnp
from jax import lax
from jax.experimental import pallas as pl
from jax.experimental.pallas import tpu as pltpu

F32 = jnp.float32
BF16 = jnp.bfloat16

D_MODEL = 1024
BATCH = 16
SEQ = 2048
N_TOK = BATCH * SEQ
N_HEADS = 16
HEAD_DIM = 64
D_ATT = N_HEADS * HEAD_DIM
SSM_GROUP = 16
N_GROUPS = 64
STATE = 64
D_FF = 2816
EPS = 1e-6

LANES = 128
ROW_TILE = 512
TILES_PER_SEQ = SEQ // ROW_TILE
SUB_TILES = (slice(0, ROW_TILE // 2), slice(ROW_TILE // 2, ROW_TILE))
ATT_TQ = 256
ATT_LOOKAHEAD = 2
ATT_ONES_ROWS = 16
LOG2E = math.log2(math.e)
HEAD_PAIRS = N_HEADS // 2
ATT_PAIRS = 2
ATT_HEADS = 2 * ATT_PAIRS
SCAN_STEPS = 32
SCAN_ROWS = SCAN_STEPS * BATCH
PERM_STEPS = 16
PERM_ROWS = PERM_STEPS * BATCH
GROUP_BLOCK = 16
N_GROUP_BLOCKS = N_GROUPS // GROUP_BLOCK
GB_CH = GROUP_BLOCK * SSM_GROUP
GB_ST = GROUP_BLOCK * STATE
MXU_DIM = 256
_FF_SPLIT = (D_FF // MXU_DIM // 2) * MXU_DIM
FF_CHUNKS = ((0, _FF_SPLIT), (_FF_SPLIT, D_FF))
VMEM_LIMIT = 56 * 1024 * 1024


def _const_spec(shape):
    nd = len(shape)
    return pl.BlockSpec(shape, lambda *_: (0,) * nd, pipeline_mode=pl.Buffered(1))


def _rmsnorm(xf, gain):
    return xf * lax.rsqrt(jnp.mean(xf * xf, axis=-1, keepdims=True) + EPS) * gain


def _dot(a, b):
    return jnp.dot(a, b, preferred_element_type=F32)


def _sigmoid(x):
    return 1.0 / (1.0 + jnp.exp(-x))


def _params(*sem):
    return pltpu.CompilerParams(dimension_semantics=sem, vmem_limit_bytes=VMEM_LIMIT)


def _fox_in_kernel(x_ref, g_ref, w_ref, wf_ref, bf_ref, qg_ref, kg_ref,
                   wg32_ref, wu32_ref, wd32_ref,
                   qt_ref, k_ref, vt_ref, sg_ref, c_ref, wg_ref, wu_ref, wd_ref, carry_ref):
    i = pl.program_id(0)
    hs = [_rmsnorm(x_ref[r, :], g_ref[...]).astype(BF16) for r in SUB_TILES]

    z = jnp.concatenate([_dot(h, wf_ref[...]) for h in hs], axis=0) + bf_ref[...]
    log_f = jnp.minimum(z, 0.0) - jnp.log(1.0 + jnp.exp(-jnp.abs(z)))
    rows = lax.broadcasted_iota(jnp.int32, (ROW_TILE, ROW_TILE), 0)
    cols = lax.broadcasted_iota(jnp.int32, (ROW_TILE, ROW_TILE), 1)
    tril = (rows >= cols).astype(BF16)
    hi = log_f.astype(BF16).astype(F32)
    r1 = log_f - hi
    mid = r1.astype(BF16).astype(F32)
    lo3 = r1 - mid
    lane = lax.broadcasted_iota(jnp.int32, (ROW_TILE, LANES), 1)
    packed = jnp.where(lane < N_HEADS, hi,
                       jnp.where(lane < 2 * N_HEADS, pltpu.roll(mid, N_HEADS, 1),
                                 pltpu.roll(lo3, 2 * N_HEADS, 1)))
    sums = _dot(tril, packed.astype(BF16))
    cum = (sums + pltpu.roll(sums, LANES - N_HEADS, 1)
           + pltpu.roll(sums, LANES - 2 * N_HEADS, 1))

    def head_norm_t(raw_t, gain_ref):
        out = []
        for hd in range(0, D_ATT, HEAD_DIM):
            rows = raw_t[hd:hd + HEAD_DIM]
            ssq = jnp.sum(rows * rows, axis=0, keepdims=True)
            out.append(rows * lax.rsqrt(ssq * (1.0 / HEAD_DIM) + EPS) * gain_ref[...])
        return jnp.concatenate(out, axis=0)

    q_raws = [_dot(h, w_ref[:, :D_ATT]) for h in hs]
    k_raws = [_dot(h, w_ref[:, D_ATT:2 * D_ATT]) for h in hs]
    wg_ref[...] = wg32_ref[...].astype(BF16)
    wu_ref[...] = wu32_ref[...].astype(BF16)
    wd_ref[...] = wd32_ref[...].astype(BF16)
    for r, q_raw in zip(SUB_TILES, q_raws):
        qt_ref[:, r] = head_norm_t(q_raw.T, qg_ref).astype(BF16)
    for r, h in zip(SUB_TILES, hs):
        vt_ref[:, r] = _dot(h, w_ref[:, 2 * D_ATT:3 * D_ATT]).T.astype(BF16)
    for r, k_raw in zip(SUB_TILES, k_raws):
        k_ref[r, :] = head_norm_t(k_raw.T, kg_ref).T.astype(BF16)
    for r, h in zip(SUB_TILES, hs):
        sg_ref[r, :] = _sigmoid(_dot(h, w_ref[:, 3 * D_ATT:4 * D_ATT])).astype(BF16)

    @pl.when(i % TILES_PER_SEQ == 0)
    def _():
        carry_ref[...] = jnp.zeros_like(carry_ref)

    cum = cum + carry_ref[...]
    c_ref[...] = cum * LOG2E
    carry_ref[...] = cum[ROW_TILE - 1:ROW_TILE, :]


def _fox_in(x, pre_gain, w_main, w_f, b_f, q_gain, k_gain, ffn_w_gate, ffn_w_up, ffn_w_down):
    n_steps = N_TOK // ROW_TILE
    row_spec = pl.BlockSpec((ROW_TILE, D_MODEL), lambda i: (i, 0))
    col_spec = pl.BlockSpec((D_ATT, ROW_TILE), lambda i: (0, i))
    act = jax.ShapeDtypeStruct((N_TOK, D_ATT), BF16)
    act_t = jax.ShapeDtypeStruct((D_ATT, N_TOK), BF16)
    gu_rows = D_MODEL // n_steps
    gu_spec = pl.BlockSpec((2, gu_rows, D_FF), lambda i: (0, i, 0))
    down_blocks = n_steps // 4
    down_rows = D_FF // down_blocks
    down_spec = pl.BlockSpec((None, down_rows, D_MODEL),
                             lambda i: (i // (2 * down_blocks), (i // 2) % down_blocks, 0))
    assert gu_rows % 16 == 0 and down_rows % 16 == 0 and down_rows * down_blocks == D_FF
    return pl.pallas_call(
        _fox_in_kernel,
        grid=(N_TOK // ROW_TILE,),
        in_specs=[
            row_spec,
            _const_spec((1, D_MODEL)),
            _layer_spec((D_MODEL, 4 * D_ATT), 0),
            _const_spec((D_MODEL, LANES)),
            _const_spec((1, LANES)),
            _const_spec((HEAD_DIM, 1)),
            _const_spec((HEAD_DIM, 1)),
            gu_spec, gu_spec, down_spec,
        ],
        out_specs=[col_spec, row_spec, col_spec, row_spec,
                   pl.BlockSpec((ROW_TILE, LANES), lambda i: (i, 0)),
                   gu_spec, gu_spec, down_spec],
        out_shape=[act_t, act, act_t, act, jax.ShapeDtypeStruct((N_TOK, LANES), F32),
                   jax.ShapeDtypeStruct(ffn_w_gate.shape, BF16),
                   jax.ShapeDtypeStruct(ffn_w_up.shape, BF16),
                   jax.ShapeDtypeStruct(ffn_w_down.shape, BF16)],
        scratch_shapes=[pltpu.VMEM((1, LANES), F32)],
        compiler_params=_params("arbitrary"),
        name="fox_in",
    )(x, pre_gain, w_main, w_f, b_f, q_gain, k_gain, ffn_w_gate, ffn_w_up, ffn_w_down)


def _attn_kernel(qt_ref, k_ref, vt_ref, c_ref, sg_ref, o_ref):
    first_head = ATT_HEADS * pl.program_id(1)
    low_rows = lax.broadcasted_iota(jnp.int32, (LANES, SEQ), 0) < HEAD_DIM
    ones = jnp.ones((ATT_ONES_ROWS, SEQ), BF16)
    c_all = c_ref[...]
    lane = lax.broadcasted_iota(jnp.int32, (SEQ, LANES), 1)
    ks, qmt, vmt, cb = [], [], [], []
    for pr in range(ATT_PAIRS):
        feat = slice(pr * LANES, (pr + 1) * LANES)
        ks.append(k_ref[:, feat])
        qt = qt_ref[feat, :].astype(F32)
        qmt += [jnp.where(low_rows, qt, 0.0).astype(BF16), jnp.where(low_rows, 0.0, qt).astype(BF16)]
    for hd in range(ATT_HEADS):
        vmt.append(jnp.concatenate([vt_ref[hd * HEAD_DIM:(hd + 1) * HEAD_DIM, :], ones], axis=0))
        col = jnp.sum(jnp.where(lane == first_head + hd, c_all, 0.0), axis=-1, keepdims=True)
        cb.append(jnp.broadcast_to(col, (SEQ, ATT_TQ)))
    key = lax.broadcasted_iota(jnp.int32, (ATT_TQ, ATT_TQ), 0)
    qry = lax.broadcasted_iota(jnp.int32, (ATT_TQ, ATT_TQ), 1)
    causal = key <= qry

    def scores(qi, hd):
        k0 = qi * ATT_TQ
        k = ks[hd // 2]
        qcols = qmt[hd][:, k0:k0 + ATT_TQ]
        s_d = _dot(k[k0:k0 + ATT_TQ], qcols)
        s_o = _dot(k[:k0], qcols) if qi else None
        return s_d, s_o

    def attend(qi, hd, s_d, s_o):
        k0 = qi * ATT_TQ
        s_d = jnp.where(causal, s_d - cb[hd][k0:k0 + ATT_TQ], -1e30)
        m = jnp.max(s_d, axis=0, keepdims=True)
        if qi:
            s_o = s_o - cb[hd][:k0]
            m = jnp.maximum(m, jnp.max(s_o, axis=0, keepdims=True))
        acc = _dot(vmt[hd][:, k0:k0 + ATT_TQ], jnp.exp2(s_d - m).astype(BF16))
        if qi:
            acc = acc + _dot(vmt[hd][:, :k0], jnp.exp2(s_o - m).astype(BF16))
        return acc[:HEAD_DIM] / acc[HEAD_DIM:HEAD_DIM + 1]

    items = [(qi, 2 * pr + hh) for pr in range(ATT_PAIRS)
             for qi in range(SEQ // ATT_TQ) for hh in range(2)]
    ahead = [scores(*it) for it in items[:ATT_LOOKAHEAD]]
    outs = []
    for n, (qi, hd) in enumerate(items):
        current = ahead.pop(0)
        if n + ATT_LOOKAHEAD < len(items):
            ahead.append(scores(*items[n + ATT_LOOKAHEAD]))
        outs.append(attend(qi, hd, *current))
        if hd % 2:
            k0 = qi * ATT_TQ
            feat = slice((hd // 2) * LANES, (hd // 2 + 1) * LANES)
            gate = sg_ref[k0:k0 + ATT_TQ, feat].astype(F32)
            o_ref[k0:k0 + ATT_TQ, feat] = (jnp.concatenate(outs, axis=0).T * gate).astype(BF16)
            outs = []


def _attention(qt, k, vt, c, sg):
    width = ATT_PAIRS * LANES
    seq_spec = pl.BlockSpec((SEQ, width), lambda b, j: (b, j))
    seq_t_spec = pl.BlockSpec((width, SEQ), lambda b, j: (j, b))
    return pl.pallas_call(
        _attn_kernel,
        grid=(BATCH, HEAD_PAIRS // ATT_PAIRS),
        in_specs=[seq_t_spec, seq_spec, seq_t_spec,
                  pl.BlockSpec((SEQ, LANES), lambda b, j: (b, 0)), seq_spec],
        out_specs=seq_spec,
        out_shape=jax.ShapeDtypeStruct((N_TOK, D_ATT), BF16),
        compiler_params=_params("arbitrary", "arbitrary"),
        name="fox_attention",
    )(qt, k, vt, c, sg)


def _ffn_body(xs, pre_ref, wg_ref, wu_ref, wd_ref, post_ref):
    hs = [_rmsnorm(x, pre_ref[...]).astype(BF16) for x in xs]
    accs = [None] * len(xs)
    for c0, c1 in FF_CHUNKS:
        for n, h in enumerate(hs):
            g = _dot(h, wg_ref[:, c0:c1])
            u = _dot(h, wu_ref[:, c0:c1])
            a = (g * _sigmoid(g) * u).astype(BF16)
            part = _dot(a, wd_ref[c0:c1, :])
            accs[n] = part if accs[n] is None else accs[n] + part
    return [x + _rmsnorm(acc, post_ref[...]) for x, acc in zip(xs, accs)]


def _row_spec():
    return pl.BlockSpec((ROW_TILE, D_MODEL), lambda i: (i, 0))


def _layer_spec(shape, layer):
    nd = len(shape)
    return pl.BlockSpec((None,) + shape, lambda *_: (layer,) + (0,) * nd,
                        pipeline_mode=pl.Buffered(1))


def _ffn_specs(layer):
    return [_const_spec((1, D_MODEL)), _layer_spec((D_MODEL, D_FF), layer),
            _layer_spec((D_MODEL, D_FF), layer), _layer_spec((D_FF, D_MODEL), layer),
            _const_spec((1, D_MODEL))]


def _layer0_tail_kernel(o_ref, x_ref, wo_ref, mpost_ref,
                        fpre_ref, wg_ref, wu_ref, wd_ref, fpost_ref,
                        spre_ref, wsin_ref, x2_ref, u_ref):
    ms = [_dot(o_ref[r, :], wo_ref[...]) for r in SUB_TILES]
    x1s = [x_ref[r, :] + _rmsnorm(m, mpost_ref[...]) for r, m in zip(SUB_TILES, ms)]
    x2s = _ffn_body(x1s, fpre_ref, wg_ref, wu_ref, wd_ref, fpost_ref)
    for r, x2 in zip(SUB_TILES, x2s):
        x2_ref[r, :] = x2
        u_ref[r, :] = _dot(_rmsnorm(x2, spre_ref[...]).astype(BF16), wsin_ref[...]).astype(BF16)


def _layer0_tail(o, x, w_out, mix_post, ffn_w, s5_pre, s5_w_in):
    return pl.pallas_call(
        _layer0_tail_kernel,
        grid=(N_TOK // ROW_TILE,),
        in_specs=[_row_spec(), _row_spec(),
                  _const_spec((D_ATT, D_MODEL)), _const_spec((1, D_MODEL)),
                  *_ffn_specs(0),
                  _const_spec((1, D_MODEL)), _const_spec((D_MODEL, D_MODEL))],
        out_specs=[_row_spec(), _row_spec()],
        out_shape=[jax.ShapeDtypeStruct((N_TOK, D_MODEL), F32),
                   jax.ShapeDtypeStruct((N_TOK, D_MODEL), BF16)],
        compiler_params=_params("arbitrary"),
        name="layer0_tail",
    )(o, x, w_out, mix_post, *ffn_w, s5_pre, s5_w_in)


def _layer1_tail_kernel(y_ref, x_ref, wglu_ref, wo_ref, mpost_ref,
                        fpre_ref, wg_ref, wu_ref, wd_ref, fpost_ref, out_ref):
    ms = []
    for r in SUB_TILES:
        y = y_ref[r, :]
        gated = (y.astype(F32) * _sigmoid(_dot(y, wglu_ref[...]))).astype(BF16)
        ms.append(_dot(gated, wo_ref[...]))
    x3s = [x_ref[r, :] + _rmsnorm(m, mpost_ref[...]) for r, m in zip(SUB_TILES, ms)]
    outs = _ffn_body(x3s, fpre_ref, wg_ref, wu_ref, wd_ref, fpost_ref)
    for r, out in zip(SUB_TILES, outs):
        out_ref[r, :] = out


def _layer1_tail(y, x, w_glu, w_out, mix_post, ffn_w):
    return pl.pallas_call(
        _layer1_tail_kernel,
        grid=(N_TOK // ROW_TILE,),
        in_specs=[_row_spec(), _row_spec(),
                  _const_spec((D_MODEL, D_MODEL)), _const_spec((D_MODEL, D_MODEL)),
                  _const_spec((1, D_MODEL)), *_ffn_specs(1)],
        out_specs=_row_spec(),
        out_shape=jax.ShapeDtypeStruct((N_TOK, D_MODEL), F32),
        compiler_params=_params("arbitrary"),
        name="layer1_tail",
    )(y, x, w_glu, w_out, mix_post, *ffn_w)


def _s5_disc_kernel(ldt_ref, lre_ref, lim_ref, bre_ref, bim_ref, cre_ref, cim_ref,
                    are_ref, aim_ref, bdbre_ref, bdbim_ref, bdcre_ref, bdcim_ref,
                    cre_t_ref, cim_t_ref):
    dt = jnp.exp(ldt_ref[...])
    lam_re = lre_ref[...]
    lam_im = lim_ref[...]
    mag = jnp.exp(lam_re * dt)
    a_re = mag * jnp.cos(lam_im * dt)
    a_im = mag * jnp.sin(lam_im * dt)
    den = lam_re * lam_re + lam_im * lam_im
    n_re = a_re - 1.0
    z_re = (n_re * lam_re + a_im * lam_im) / den
    z_im = (a_im * lam_re - n_re * lam_im) / den
    b_re = bre_ref[...]
    b_im = bim_ref[...]
    planes = (
        (bdbre_ref, z_re * b_re - z_im * b_im),
        (bdbim_ref, z_re * b_im + z_im * b_re),
        (cre_t_ref, cre_ref[...]),
        (cim_t_ref, cim_ref[...]),
    )
    for dst, _ in planes:
        dst[...] = jnp.zeros_like(dst)
    zeros = jnp.zeros((SSM_GROUP, STATE), F32)
    for blk in range(N_GROUP_BLOCKS):
        for m in range(GROUP_BLOCK // 2):
            g0 = blk * GROUP_BLOCK + 2 * m
            rows = slice(2 * m * SSM_GROUP, 2 * (m + 1) * SSM_GROUP)
            cols = slice(2 * m * STATE, 2 * (m + 1) * STATE)
            for dst, src in planes:
                top = jnp.concatenate([src[g0], zeros], axis=1)
                bot = jnp.concatenate([zeros, src[g0 + 1]], axis=1)
                dst[blk, rows, cols] = jnp.concatenate([top, bot], axis=0).astype(dst.dtype)
            are_ref[blk, :, cols] = jnp.concatenate([a_re[g0], a_re[g0 + 1]], axis=1)
            aim_ref[blk, :, cols] = jnp.concatenate([a_im[g0], a_im[g0 + 1]], axis=1)
        bdcre_ref[blk] = cre_t_ref[blk].T.astype(BF16)
        bdcim_ref[blk] = cim_t_ref[blk].T.astype(BF16)


def _s5_discretise(log_dt, lam_re, lam_im, b_re_t, b_im_t, c_re, c_im):
    a_shape = jax.ShapeDtypeStruct((N_GROUP_BLOCKS, 1, GB_ST), F32)
    b_shape = jax.ShapeDtypeStruct((N_GROUP_BLOCKS, GB_CH, GB_ST), BF16)
    c_shape = jax.ShapeDtypeStruct((N_GROUP_BLOCKS, GB_ST, GB_CH), BF16)
    return pl.pallas_call(
        _s5_disc_kernel,
        out_shape=[a_shape, a_shape, b_shape, b_shape, c_shape, c_shape],
        scratch_shapes=[pltpu.VMEM((N_GROUP_BLOCKS, GB_CH, GB_ST), F32),
                        pltpu.VMEM((N_GROUP_BLOCKS, GB_CH, GB_ST), F32)],
        compiler_params=_params(),
        name="s5_discretise",
    )(log_dt, lam_re, lam_im, b_re_t, b_im_t, c_re, c_im)


def _s5_scan_kernel(u_ref, bre_ref, bim_ref, cre_ref, cim_ref, are_ref, aim_ref, d_ref,
                    y_ref, p_ref, pt_ref, xr_ref, xi_ref, sr_ref, si_ref):
    @pl.when(pl.program_id(0) == 0)
    def _():
        sr_ref[...] = jnp.zeros_like(sr_ref)
        si_ref[...] = jnp.zeros_like(si_ref)
        r = lax.broadcasted_iota(jnp.int32, (PERM_ROWS, PERM_ROWS), 0)
        c = lax.broadcasted_iota(jnp.int32, (PERM_ROWS, PERM_ROWS), 1)
        bits = BATCH.bit_length() - 1
        p_ref[...] = (c == (r & (BATCH - 1)) * PERM_STEPS + (r >> bits)).astype(BF16)
        pt_ref[...] = (r == (c & (BATCH - 1)) * PERM_STEPS + (c >> bits)).astype(BF16)

    u_f32 = jnp.concatenate(
        [_dot(p_ref[...], u_ref[:, t0:t0 + PERM_STEPS, :].reshape(PERM_ROWS, D_MODEL))
         for t0 in range(0, SCAN_STEPS, PERM_STEPS)], axis=0)
    u = u_f32.astype(BF16)

    def channels(g):
        return slice(g * GB_CH, (g + 1) * GB_CH)

    def drive(g):
        xr_ref[g] = _dot(u[:, channels(g)], bre_ref[g])
        xi_ref[g] = _dot(u[:, channels(g)], bim_ref[g])

    def recur(g):
        a_re = jnp.broadcast_to(are_ref[g], (BATCH, GB_ST))
        a_im = jnp.broadcast_to(aim_ref[g], (BATCH, GB_ST))
        s_re = sr_ref[g]
        s_im = si_ref[g]
        for t in range(SCAN_STEPS):
            rows = slice(t * BATCH, (t + 1) * BATCH)
            n_re = a_re * s_re - a_im * s_im + xr_ref[g, rows, :]
            n_im = a_re * s_im + a_im * s_re + xi_ref[g, rows, :]
            xr_ref[g, rows, :] = n_re
            xi_ref[g, rows, :] = n_im
            s_re, s_im = n_re, n_im
        sr_ref[g] = s_re
        si_ref[g] = s_im

    def readout(g):
        ch = channels(g)
        y = (_dot(xr_ref[g].astype(BF16), cre_ref[g])
             - _dot(xi_ref[g].astype(BF16), cim_ref[g]))
        y = y + d_ref[:, ch] * u_f32[:, ch]
        y = 0.5 * y * (1.0 + jnp.tanh(math.sqrt(2.0 / math.pi) * (y + 0.044715 * (y * y * y))))
        y = y.astype(BF16)
        for t0 in range(0, SCAN_STEPS, PERM_STEPS):
            y_bm = _dot(pt_ref[...], y[t0 * BATCH:(t0 + PERM_STEPS) * BATCH]).astype(BF16)
            y_ref[:, t0:t0 + PERM_STEPS, ch] = y_bm.reshape(BATCH, PERM_STEPS, GB_CH)

    drive(0)
    for g in range(N_GROUP_BLOCKS):
        if g + 1 < N_GROUP_BLOCKS:
            drive(g + 1)
        recur(g)
        readout(g)


def _s5_scan(u, b_re, b_im, c_re, c_im, a_re, a_im, d_skip):
    tile = pl.BlockSpec((BATCH, SCAN_STEPS, D_MODEL), lambda t: (0, t, 0))
    b_shape = (N_GROUP_BLOCKS, GB_CH, GB_ST)
    c_shape = (N_GROUP_BLOCKS, GB_ST, GB_CH)
    a_shape = (N_GROUP_BLOCKS, 1, GB_ST)
    return pl.pallas_call(
        _s5_scan_kernel,
        grid=(SEQ // SCAN_STEPS,),
        in_specs=[tile,
                  _const_spec(b_shape), _const_spec(b_shape),
                  _const_spec(c_shape), _const_spec(c_shape),
                  _const_spec(a_shape), _const_spec(a_shape),
                  _const_spec((1, D_MODEL))],
        out_specs=tile,
        out_shape=jax.ShapeDtypeStruct((BATCH, SEQ, D_MODEL), BF16),
        scratch_shapes=[pltpu.VMEM((PERM_ROWS, PERM_ROWS), BF16),
                        pltpu.VMEM((PERM_ROWS, PERM_ROWS), BF16),
                        pltpu.VMEM((N_GROUP_BLOCKS, SCAN_ROWS, GB_ST), F32),
                        pltpu.VMEM((N_GROUP_BLOCKS, SCAN_ROWS, GB_ST), F32),
                        pltpu.VMEM((N_GROUP_BLOCKS, BATCH, GB_ST), F32),
                        pltpu.VMEM((N_GROUP_BLOCKS, BATCH, GB_ST), F32)],
        compiler_params=_params("arbitrary"),
        name="s5_scan",
    )(u.reshape(BATCH, SEQ, D_MODEL), b_re, b_im, c_re, c_im, a_re, a_im, d_skip)


def kernel(x, fox_w_in, fox_b_f, fox_q_gain, fox_k_gain, fox_w_out, s5_w_in, s5_log_dt, s5_lam_re, s5_lam_im, s5_b_re, s5_b_im, s5_c_re, s5_c_im, s5_d, s5_w_glu, s5_w_out, mix_pre_gain, mix_post_gain, ffn_pre_gain, ffn_post_gain, ffn_w_gate, ffn_w_up, ffn_w_down):
    row = lambda a: a.reshape(1, -1).astype(F32)
    x0 = x.reshape(N_TOK, D_MODEL)

    w_main = fox_w_in.astype(BF16)
    w_f = jnp.pad(fox_w_in[0][:, 4 * D_ATT:], ((0, 0), (0, LANES - N_HEADS))).astype(BF16)
    b_f = jnp.pad(fox_b_f[0], (0, LANES - N_HEADS)).reshape(1, LANES).astype(F32)
    q_gain = (fox_q_gain[0] * (HEAD_DIM ** -0.5 * LOG2E)).reshape(HEAD_DIM, 1).astype(F32)
    k_gain = fox_k_gain[0].reshape(HEAD_DIM, 1).astype(F32)

    qt, k, vt, sg, c, w_gate, w_up, w_down = _fox_in(
        x0, row(mix_pre_gain[0]), w_main, w_f, b_f, q_gain, k_gain,
        ffn_w_gate, ffn_w_up, ffn_w_down)

    def ffn_weights(i):
        return (row(ffn_pre_gain[i]), w_gate, w_up, w_down, row(ffn_post_gain[i]))

    o = _attention(qt, k, vt, c, sg)
    x2, u = _layer0_tail(o, x0,fox_w_out[0].astype(BF16), row(mix_post_gain[0]),
                         ffn_weights(0), row(mix_pre_gain[1]), s5_w_in[0].astype(BF16))

    gps = (N_GROUPS, 1, STATE)
    a_re, a_im, bd_b_re, bd_b_im, bd_c_re, bd_c_im = _s5_discretise(
        s5_log_dt[0].reshape(N_GROUPS, 1, 1).astype(F32),
        s5_lam_re[0].reshape(gps).astype(F32), s5_lam_im[0].reshape(gps).astype(F32),
        s5_b_re[0].transpose(0, 2, 1).astype(F32), s5_b_im[0].transpose(0, 2, 1).astype(F32),
        s5_c_re[0].astype(F32), s5_c_im[0].astype(F32))
    y = _s5_scan(u, bd_b_re, bd_b_im, bd_c_re, bd_c_im, a_re, a_im, row(s5_d[0]))
    x4 = _layer1_tail(y.reshape(N_TOK, D_MODEL), x2, s5_w_glu[0].astype(BF16),
                      s5_w_out[0].astype(BF16), row(mix_post_gain[1]), ffn_weights(1))
    return x4.reshape(BATCH, SEQ, D_MODEL)
```
